```python
import math
import jax, jax.numpy as jnp
from jax import lax
import numpy as np

D_MODEL = 1024
BATCH = 8
SEQ = 2048
DEPTH = 4

HEAD_DIM = 64
DIL_GROUPS = ((128, 1), (512, 4), (2048, 16))
N_DIL_GROUPS = 3
HEADS_PER_GROUP = 8
DIL_WIDTH = N_DIL_GROUPS * HEADS_PER_GROUP * HEAD_DIM
DIL_OUT = HEADS_PER_GROUP * HEAD_DIM
DIFF_HEADS = D_MODEL // (2 * HEAD_DIM)
DIFF_QK = DIFF_HEADS * 2 * HEAD_DIM
DIFF_V = DIFF_HEADS * 2 * HEAD_DIM
BLK = 128
NUM_BUCKETS = 32
MAX_DISTANCE = 2048
N_BIAS_HEADS = N_DIL_GROUPS * HEADS_PER_GROUP + DIFF_HEADS
IN_WIDTH = 3 * DIL_WIDTH + 2 * DIFF_QK + DIFF_V + 2 * D_MODEL
PEER_HEADS = 8
PEER_DQ = 256
PEER_DK = PEER_DQ // 2
N_KEYS = 128
N_EXPERTS = N_KEYS * N_KEYS
PEER_TOPK = 16
PEER_CHUNK = 128
NORM_EPS = 1e-6
SUBLN_EPS = 1e-5
NEG_INF = -1e30

kernel_name = 'hybrid_dilated_diff_peer_block'


def rmsnorm(x, g, eps=NORM_EPS):
    xf = x.astype(jnp.float32)
    y = xf * lax.rsqrt(jnp.mean(xf * xf, axis=-1, keepdims=True) + eps)
    return (y * g.astype(jnp.float32)).astype(x.dtype)


def t5_bucket(dist):
    max_exact = NUM_BUCKETS // 2
    d = jnp.maximum(dist, 0)
    ratio = jnp.log(jnp.maximum(d, 1).astype(jnp.float32) / max_exact) / math.log(MAX_DISTANCE / max_exact)
    large = jnp.minimum(max_exact + (ratio * (NUM_BUCKETS - max_exact)).astype(jnp.int32), NUM_BUCKETS - 1)
    return jnp.where(d < max_exact, d, large)


def dilated_group(q, k, v, bias_tab, n_back, r):
    b, s, h, dh = q.shape
    L = s // r
    Lp = -(-L // BLK) * BLK
    nb = Lp // BLK

    def strided(a):
        a = a.reshape(b, L, r, h, dh).transpose(0, 2, 1, 3, 4)
        a = jnp.pad(a, ((0, 0), (0, 0), (0, Lp - L), (0, 0), (0, 0)))
        return a.reshape(b, r, nb, BLK, h, dh)

    def with_prev(a):
        prev = jnp.pad(a, ((0, 0), (0, 0), (1, 0), (0, 0), (0, 0), (0, 0)))[:, :, :nb]
        return jnp.concatenate([prev, a], axis=3)

    qs, ks, vs = strided(q), strided(k), strided(v)
    kw, vw = with_prev(ks), with_prev(vs)
    qi = jnp.arange(BLK)[:, None]
    kj = jnp.arange(2 * BLK)[None, :]
    delta = qi + BLK - kj
    blk = jnp.arange(nb)[:, None, None]
    valid = (delta >= 0) & (delta <= n_back) & ((blk > 0) | (kj >= BLK))
    bias = bias_tab[t5_bucket(delta * r)].astype(jnp.float32).transpose(2, 0, 1)
    logits = jnp.einsum('bmnqhd,bmnkhd->bmnhqk', qs, kw).astype(jnp.float32) * (dh ** -0.5) + bias
    logits = jnp.where(valid[:, None], logits, NEG_INF)
    mx = jnp.max(logits, axis=-1, keepdims=True)
    p = jnp.exp(logits - mx)
    den = jnp.sum(p, axis=-1, keepdims=True)
    o = jnp.einsum('bmnhqk,bmnkhd->bmnqhd', p / den, vw.astype(jnp.float32))
    lse = (mx + jnp.log(den))[..., 0].transpose(0, 1, 2, 4, 3)

    def unstrided(a):
        a = a.reshape((b, r, Lp) + a.shape[4:])[:, :, :L]
        a = jnp.swapaxes(a, 1, 2)
        return a.reshape((b, s) + a.shape[3:])

    return unstrided(o), unstrided(lse)


def diff_attention(q, k, v, bias_tab, lam, subln_g, lam_init):
    b, s, h, _, dh = q.shape
    nb = s // BLK
    kpos = jnp.arange(s)
    qblocks = jnp.moveaxis(q.reshape(b, nb, BLK, h, 2, dh), 1, 0)
    vf = v.astype(jnp.float32)

    def one_block(args):
        qb, n = args
        qpos = n * BLK + jnp.arange(BLK)
        dist = qpos[:, None] - kpos[None, :]
        bias = bias_tab[t5_bucket(dist)].astype(jnp.float32).transpose(2, 0, 1)
        logits = jnp.einsum('bqhtd,bkhtd->bthqk', qb, k).astype(jnp.float32) * (dh ** -0.5) + bias
        logits = jnp.where(dist >= 0, logits, NEG_INF)
        att = jax.nn.softmax(logits, axis=-1)
        w = att[:, 0] - lam * att[:, 1]
        return jnp.einsum('bhqk,bkhd->bqhd', w, vf)

    o = lax.map(one_block, (qblocks, jnp.arange(nb)))
    o = jnp.moveaxis(o, 0, 1).reshape(b, s, h, 2 * dh)
    o = rmsnorm(o, subln_g, SUBLN_EPS) * (1.0 - lam_init)
    return o.reshape(b, s, h * 2 * dh).astype(q.dtype)


def token_mixer(h, w_in, w_pa, w_pb, w_o, lq1, lk1, lq2, lk2, subln_g, rel_bias, lam_init):
    b, s, _ = h.shape
    proj = h @ w_in
    sizes = [DIL_WIDTH, DIL_WIDTH, DIL_WIDTH, DIFF_QK, DIFF_QK, DIFF_V, D_MODEL]
    cuts = [int(i) for i in np.cumsum(sizes)]
    qa, ka, va, qb, kb, vb, ga, gb = jnp.split(proj, cuts, axis=-1)
    shp_a = (b, s, N_DIL_GROUPS, HEADS_PER_GROUP, HEAD_DIM)
    qa, ka, va = qa.reshape(shp_a), ka.reshape(shp_a), va.reshape(shp_a)
    outs, lses = [], []
    for g, (window, dil) in enumerate(DIL_GROUPS):
        tab = rel_bias[:, g * HEADS_PER_GROUP:(g + 1) * HEADS_PER_GROUP]
        o, lse = dilated_group(qa[:, :, g], ka[:, :, g], va[:, :, g], tab, window // dil, dil)
        outs.append(o)
        lses.append(lse)
    wts = jax.nn.softmax(jnp.stack(lses, 0), axis=0)
    oa = jnp.sum(wts[..., None] * jnp.stack(outs, 0), axis=0).reshape(b, s, DIL_OUT).astype(h.dtype)
    lam = jnp.exp(jnp.sum(lq1 * lk1)) - jnp.exp(jnp.sum(lq2 * lk2)) + lam_init
    ob = diff_attention(qb.reshape(b, s, DIFF_HEADS, 2, HEAD_DIM), kb.reshape(b, s, DIFF_HEADS, 2, HEAD_DIM),
                        vb.reshape(b, s, DIFF_HEADS, 2 * HEAD_DIM), rel_bias[:, N_DIL_GROUPS * HEADS_PER_GROUP:],
                        lam, subln_g, lam_init)
    merged = jax.nn.sigmoid(ga) * (oa @ w_pa) + jax.nn.sigmoid(gb) * (ob @ w_pb)
    return merged @ w_o


def peer(h, w_q, sub_keys, u_tab, v_tab):
    b, s, d = h.shape
    t = b * s
    xt = h.reshape(t, d)
    q = (xt @ w_q).reshape(t, PEER_HEADS, 2, PEER_DK)
    sc = jnp.einsum('thpd,pkd->thpk', q, sub_keys).astype(jnp.float32)
    top_s, top_i = lax.top_k(sc, PEER_TOPK)
    cand_s = (top_s[:, :, 0, :, None] + top_s[:, :, 1, None, :]).reshape(t, PEER_HEADS, PEER_TOPK * PEER_TOPK)
    cand_id = (top_i[:, :, 0, :, None] * N_KEYS + top_i[:, :, 1, None, :]).reshape(t, PEER_HEADS, PEER_TOPK * PEER_TOPK)
    best_s, best_j = lax.top_k(cand_s, PEER_TOPK)
    ids = jnp.take_along_axis(cand_id, best_j, axis=-1)
    gates = jax.nn.softmax(best_s, axis=-1)
    nc = t // PEER_CHUNK

    def chunk(args):
        xc, idc, gc = args
        u = u_tab[idc]
        act = jax.nn.gelu(jnp.einsum('cd,chkd->chk', xc, u).astype(jnp.float32))
        vv = v_tab[idc]
        return jnp.einsum('chk,chkd->cd', (gc * act).astype(xc.dtype), vv)

    out = lax.map(chunk, (xt.reshape(nc, PEER_CHUNK, d),
                          ids.reshape(nc, PEER_CHUNK, PEER_HEADS, PEER_TOPK),
                          gates.reshape(nc, PEER_CHUNK, PEER_HEADS, PEER_TOPK)))
    return out.reshape(b, s, d)


def setup_inputs(seed: int = 0) -> dict:
    key = jax.random.key(seed)
    ks = jax.random.split(key, 21)
    f32 = jnp.float32

    def nrm(k, shape, std):
        return jax.random.normal(k, shape, f32) * std

    x = nrm(ks[0], (BATCH, SEQ, D_MODEL), 1.0)
    c = nrm(ks[1], (BATCH, D_MODEL), 1.0)
    w_ada = nrm(ks[2], (DEPTH, D_MODEL, 6 * D_MODEL), 0.5 * D_MODEL ** -0.5)
    b_ada = nrm(ks[3], (DEPTH, 6 * D_MODEL), 0.02)
    norm1_g = 1.0 + nrm(ks[4], (DEPTH, D_MODEL), 0.02)
    norm2_g = 1.0 + nrm(ks[5], (DEPTH, D_MODEL), 0.02)
    w_in = nrm(ks[6], (DEPTH, D_MODEL, IN_WIDTH), D_MODEL ** -0.5)
    w_proj_a = nrm(ks[7], (DEPTH, DIL_OUT, D_MODEL), DIL_OUT ** -0.5)
    w_proj_b = nrm(ks[8], (DEPTH, DIFF_V, D_MODEL), DIFF_V ** -0.5)
    w_out = nrm(ks[9], (DEPTH, D_MODEL, D_MODEL), D_MODEL ** -0.5)
    lam_q1 = nrm(ks[10], (DEPTH, HEAD_DIM), 0.1)
    lam_k1 = nrm(ks[11], (DEPTH, HEAD_DIM), 0.1)
    lam_q2 = nrm(ks[12], (DEPTH, HEAD_DIM), 0.1)
    lam_k2 = nrm(ks[13], (DEPTH, HEAD_DIM), 0.1)
    subln_g = 1.0 + nrm(ks[14], (DEPTH, 2 * HEAD_DIM), 0.02)
    rel_bias = nrm(ks[15], (NUM_BUCKETS, N_BIAS_HEADS), 0.5)
    peer_wq = nrm(ks[16], (DEPTH, D_MODEL, PEER_HEADS * PEER_DQ), D_MODEL ** -0.5)
    peer_subkeys = nrm(ks[17], (DEPTH, 2, N_KEYS, PEER_DK), PEER_DK ** -0.5)
    peer_u = nrm(ks[18], (DEPTH, N_EXPERTS, D_MODEL), D_MODEL ** -0.5)
    peer_v = nrm(ks[19], (DEPTH, N_EXPERTS, D_MODEL), PEER_HEADS ** -0.5)
    final_g = 1.0 + nrm(ks[20], (D_MODEL,), 0.02)
    return {'x': x, 'c': c, 'w_ada': w_ada, 'b_ada': b_ada, 'norm1_g': norm1_g, 'norm2_g': norm2_g,
            'w_in': w_in, 'w_proj_a': w_proj_a, 'w_proj_b': w_proj_b, 'w_out': w_out,
            'lam_q1': lam_q1, 'lam_k1': lam_k1, 'lam_q2': lam_q2, 'lam_k2': lam_k2, 'subln_g': subln_g,
            'rel_bias': rel_bias, 'peer_wq': peer_wq, 'peer_subkeys': peer_subkeys,
            'peer_u': peer_u, 'peer_v': peer_v, 'final_g': final_g}


def reference(x, c, w_ada, b_ada, norm1_g, norm2_g, w_in, w_proj_a, w_proj_b, w_out,
              lam_q1, lam_k1, lam_q2, lam_k2, subln_g, rel_bias, peer_wq, peer_subkeys,
              peer_u, peer_v, final_g):
    cond = jax.nn.silu(c)
    for l in range(DEPTH):
        lam_init = 0.8 - 0.6 * math.exp(-0.3 * l)
        mod = (cond @ w_ada[l] + b_ada[l])[:, None, :]
        sh1, sc1, g1, sh2, sc2, g2 = jnp.split(mod, 6, axis=-1)
        h = rmsnorm(x, norm1_g[l]) * (1.0 + sc1) + sh1
        x = x + g1 * token_mixer(h, w_in[l], w_proj_a[l], w_proj_b[l], w_out[l],
                                 lam_q1[l], lam_k1[l], lam_q2[l], lam_k2[l], subln_g[l], rel_bias, lam_init)
        h = rmsnorm(x, norm2_g[l]) * (1.0 + sc2) + sh2
        x = x + g2 * peer(h, peer_wq[l], peer_subkeys[l], peer_u[l], peer_v[l])
    return rmsnorm(x, final_g)
```

```python
import functools
import math

import numpy as np
import jax
import jax.numpy as jnp
from jax import lax
from jax.experimental import pallas as pl
from jax.experimental.pallas import tpu as pltpu

F32 = jnp.float32
BF16 = jnp.bfloat16

D_MODEL = 1024
HEAD_DIM = 64
DIL_GROUPS = ((128, 1), (512, 4), (2048, 16))
N_DIL_GROUPS = 3
HEADS_PER_GROUP = 8
GROUP_W = HEADS_PER_GROUP * HEAD_DIM
DIL_WIDTH = N_DIL_GROUPS * GROUP_W
DIFF_HEADS = 8
DIFF_W = DIFF_HEADS * 2 * HEAD_DIM
BLK = 128
NUM_BUCKETS = 32
MAX_DISTANCE = 2048
IN_WIDTH = 3 * DIL_WIDTH + 3 * DIFF_W + 2 * D_MODEL
PEER_HEADS = 8
PEER_DK = 128
N_KEYS = 128
PEER_TOPK = 16
NORM_EPS = 1e-6
SUBLN_EPS = 1e-5
NEG_INF = -1e30
Q_SCALE = HEAD_DIM ** -0.5

LANES = 128
VMEM_LIMIT = 56 * 1024 * 1024

NT_DIMS = (((1,), (1,)), ((), ()))


def _params(sem, vmem=VMEM_LIMIT):
    return pltpu.CompilerParams(dimension_semantics=sem, vmem_limit_bytes=vmem)


def _resident(shape, index_map):
    return pl.BlockSpec(shape, index_map, pipeline_mode=pl.Buffered(1))


def _ada_kernel(c_ref, w_ref, b_ref, o_ref):
    c = c_ref[...]
    cond = c * jax.nn.sigmoid(c)
    o_ref[0] = jnp.dot(cond, w_ref[0], preferred_element_type=F32) + b_ref[0]


def _ada_call(c, w_ada, b_ada):
    depth, d, n6 = w_ada.shape
    b = c.shape[0]
    nchunk = n6 // d
    return pl.pallas_call(
        _ada_kernel,
        grid=(depth, nchunk),
        in_specs=[pl.BlockSpec((b, d), lambda l, j: (0, 0)),
                  pl.BlockSpec((1, d, d), lambda l, j: (l, 0, j)),
                  pl.BlockSpec((1, 1, d), lambda l, j: (l, 0, j))],
        out_specs=pl.BlockSpec((1, b, d), lambda l, j: (l, 0, j)),
        out_shape=jax.ShapeDtypeStruct((depth, b, n6), F32),
        compiler_params=_params(("arbitrary", "arbitrary")),
        name="ada_mod",
    )(c, w_ada, b_ada.reshape(depth, 1, n6))


def _modulated_norm(x, g, sc, sh):
    ms = jnp.mean(x * x, axis=-1, keepdims=True)
    return (x * lax.rsqrt(ms + NORM_EPS) * g) * (1.0 + sc) + sh


def _in_kernel(x_ref, g_ref, sc_ref, sh_ref, w_ref, p1_ref, p2_ref, p3_ref, qkv_ref, gate_ref):
    hb = _modulated_norm(x_ref[...], g_ref[...], sc_ref[0], sh_ref[0]).astype(BF16)

    def mm(c0, scale=None):
        r = jnp.dot(hb, w_ref[:, c0:c0 + GROUP_W], preferred_element_type=F32)
        if scale is not None:
            r = r * scale
        return r.astype(BF16)

    for g, p_ref in enumerate((p1_ref, p2_ref, p3_ref)):
        p_ref[:, 0:GROUP_W] = mm(g * GROUP_W, Q_SCALE)
        p_ref[:, GROUP_W:2 * GROUP_W] = mm(DIL_WIDTH + g * GROUP_W)
        p_ref[:, 2 * GROUP_W:3 * GROUP_W] = mm(2 * DIL_WIDTH + g * GROUP_W)
    base = 3 * DIL_WIDTH
    for j in range(3 * DIFF_W // GROUP_W):
        qkv_ref[:, j * GROUP_W:(j + 1) * GROUP_W] = mm(base + j * GROUP_W, Q_SCALE if j < DIFF_W // GROUP_W else None)
    base += 3 * DIFF_W
    for j in range(2 * D_MODEL // GROUP_W):
        gate_ref[:, j * GROUP_W:(j + 1) * GROUP_W] = mm(base + j * GROUP_W)


def _in_call(x2, mod_l, g1, w_in_bf, seq, tm):
    t, d = x2.shape
    nper = seq // tm
    mod_spec = lambda k: pl.BlockSpec((1, 1, d), lambda i: (i // nper, 0, k))
    row = lambda w: pl.BlockSpec((tm, w), lambda i: (i, 0))
    return pl.pallas_call(
        _in_kernel,
        grid=(t // tm,),
        in_specs=[row(d), pl.BlockSpec((1, d), lambda i: (0, 0)), mod_spec(1), mod_spec(0),
                  _resident((d, IN_WIDTH), lambda i: (0, 0))],
        out_specs=[row(3 * GROUP_W), row(3 * GROUP_W), row(3 * GROUP_W), row(3 * DIFF_W), row(2 * D_MODEL)],
        out_shape=[jax.ShapeDtypeStruct((t, 3 * GROUP_W), BF16)] * 3
        + [jax.ShapeDtypeStruct((t, 3 * DIFF_W), BF16), jax.ShapeDtypeStruct((t, 2 * D_MODEL), BF16)],
        compiler_params=_params(("arbitrary",)),
        name="in_proj",
    )(x2, g1, mod_l, mod_l, w_in_bf)


def _t5_bucket(dist):
    max_exact = NUM_BUCKETS // 2
    d = jnp.maximum(dist, 0)
    ratio = jnp.log(jnp.maximum(d, 1).astype(F32) / max_exact) / math.log(MAX_DISTANCE / max_exact)
    large = jnp.minimum(max_exact + (ratio * (NUM_BUCKETS - max_exact)).astype(jnp.int32), NUM_BUCKETS - 1)
    return jnp.where(d < max_exact, d, large)


def _dil_bias_table(rel_bias, g, r):
    qi = jnp.arange(BLK)[:, None]
    kj = jnp.arange(2 * BLK)[None, :]
    delta = qi + BLK - kj
    n_back = DIL_GROUPS[g][0] // r
    tab = rel_bias[:, g * HEADS_PER_GROUP:(g + 1) * HEADS_PER_GROUP]
    bias = tab[_t5_bucket(delta * r)].astype(F32).transpose(2, 0, 1)
    valid = (delta >= 0) & (delta <= n_back)
    return jnp.where(valid[None], bias, NEG_INF)


def _diff_bias_table(rel_bias, tq, nd):
    i = jnp.arange(tq)[:, None]
    j = jnp.arange(tq)[None, :]
    dist = jnp.arange(nd)[:, None, None] * tq + (i - j)[None]
    tab = rel_bias[:, N_DIL_GROUPS * HEADS_PER_GROUP:]
    bias = tab[_t5_bucket(dist)].astype(F32).transpose(3, 0, 1, 2)
    return jnp.where((dist >= 0)[None], bias, NEG_INF)


def _dil_pair(q2, kc, vc, kp, vp, bias_ref, pair):
    lane = lax.broadcasted_iota(jnp.int32, (BLK, LANES), 1)
    low = lane < HEAD_DIM
    outs, lses = [], []
    for sub in range(2):
        head = 2 * pair + sub
        qm = jnp.where(low if sub == 0 else jnp.logical_not(low), q2, jnp.zeros_like(q2))
        sc = lax.dot_general(qm, kc, NT_DIMS, preferred_element_type=F32) + bias_ref[head, :, BLK:2 * BLK]
        mx = jnp.max(sc, axis=-1, keepdims=True)
        if kp is not None:
            sp = lax.dot_general(qm, kp, NT_DIMS, preferred_element_type=F32) + bias_ref[head, :, 0:BLK]
            mx = jnp.maximum(mx, jnp.max(sp, axis=-1, keepdims=True))
        pc = jnp.exp(sc - mx)
        den = jnp.sum(pc, axis=-1, keepdims=True)
        acc = jnp.dot(pc.astype(BF16), vc, preferred_element_type=F32)
        if kp is not None:
            pp = jnp.exp(sp - mx)
            den = den + jnp.sum(pp, axis=-1, keepdims=True)
            acc = acc + jnp.dot(pp.astype(BF16), vp, preferred_element_type=F32)
        outs.append(acc / den)
        lses.append(mx + jnp.log(den))
    o = jnp.where(low, outs[0], outs[1])
    lse = jnp.where(low, lses[0], lses[1])
    return o, lse


def _dil_kernel(p_ref, bias_ref, o_ref, lse_ref, *, r, nb):
    pw = 3 * GROUP_W
    for m in range(r):
        for pair in range(HEADS_PER_GROUP // 2):
            qc = m * pw + pair * LANES
            kc_ = qc + GROUP_W
            vc_ = qc + 2 * GROUP_W
            oc = m * GROUP_W + pair * LANES

            def tile(row, prev_row):
                q2 = p_ref[0, pl.ds(row, BLK), qc:qc + LANES]
                kc = p_ref[0, pl.ds(row, BLK), kc_:kc_ + LANES]
                vc = p_ref[0, pl.ds(row, BLK), vc_:vc_ + LANES]
                if prev_row is None:
                    kp = vp = None
                else:
                    kp = p_ref[0, pl.ds(prev_row, BLK), kc_:kc_ + LANES]
                    vp = p_ref[0, pl.ds(prev_row, BLK), vc_:vc_ + LANES]
                o, lse = _dil_pair(q2, kc, vc, kp, vp, bias_ref, pair)
                o_ref[0, pl.ds(row, BLK), oc:oc + LANES] = o.astype(BF16)
                lse_ref[0, pl.ds(row, BLK), oc:oc + LANES] = lse

            tile(0, None)
            if nb > 1:
                def body(n, carry):
                    row = pl.multiple_of(n * BLK, BLK)
                    tile(row, pl.multiple_of(row - BLK, BLK))
                    return carry
                lax.fori_loop(1, nb, body, 0)


def _dil_call(p, bias, batch, seq, r):
    length = seq // r
    nb = length // BLK
    pv = p.reshape(batch, length, r * 3 * GROUP_W)
    o, lse = pl.pallas_call(
        functools.partial(_dil_kernel, r=r, nb=nb),
        grid=(batch,),
        in_specs=[pl.BlockSpec((1, length, r * 3 * GROUP_W), lambda b: (b, 0, 0)),
                  _resident(bias.shape, lambda b: (0, 0, 0))],
        out_specs=[pl.BlockSpec((1, length, r * GROUP_W), lambda b: (b, 0, 0))] * 2,
        out_shape=[jax.ShapeDtypeStruct((batch, length, r * GROUP_W), BF16),
                   jax.ShapeDtypeStruct((batch, length, r * GROUP_W), F32)],
        compiler_params=_params(("arbitrary",)),
        name=f"dilated_r{r}",
    )(pv, bias)
    return o.reshape(batch * seq, GROUP_W), lse.reshape(batch * seq, GROUP_W)


def _diff_kernel(lam_ref, q_ref, k_ref, v_ref, bias_ref, sg_ref, o_ref, m_s, l_s, acc_s, *, lam_init):
    qi = pl.program_id(2)
    ki = pl.program_id(3)

    @pl.when(ki == 0)
    def _():
        m_s[...] = jnp.full(m_s.shape, NEG_INF, F32)
        l_s[...] = jnp.zeros(l_s.shape, F32)
        acc_s[...] = jnp.zeros(acc_s.shape, F32)

    @pl.when(ki <= qi)
    def _():
        q = q_ref[0]
        k = k_ref[0]
        v = v_ref[0]
        bias = bias_ref[0, qi - ki]
        lane = lax.broadcasted_iota(jnp.int32, q.shape, 1)
        for t in range(2):
            qm = jnp.where((lane < HEAD_DIM) if t == 0 else (lane >= HEAD_DIM), q, jnp.zeros_like(q))
            s = lax.dot_general(qm, k, NT_DIMS, preferred_element_type=F32) + bias
            m_prev = m_s[t]
            m_new = jnp.maximum(m_prev, jnp.max(s, axis=-1, keepdims=True))
            alpha = jnp.exp(m_prev - m_new)
            p = jnp.exp(s - m_new)
            l_s[t] = alpha * l_s[t] + jnp.sum(p, axis=-1, keepdims=True)
            acc_s[t] = alpha * acc_s[t] + jnp.dot(p.astype(BF16), v, preferred_element_type=F32)
            m_s[t] = m_new

    @pl.when(ki == qi)
    def _():
        lv = lam_ref[...]
        lam = (jnp.exp(jnp.sum(lv[0:1] * lv[1:2], axis=-1, keepdims=True))
               - jnp.exp(jnp.sum(lv[2:3] * lv[3:4], axis=-1, keepdims=True)) + lam_init)
        o = acc_s[0] / l_s[0] - lam * (acc_s[1] / l_s[1])
        ms = jnp.mean(o * o, axis=-1, keepdims=True)
        o = (o * lax.rsqrt(ms + SUBLN_EPS) * sg_ref[...]) * (1.0 - lam_init)
        o_ref[0] = o.astype(BF16)


def _diff_call(qkv, bias, lam_vecs, subln_g, batch, seq, tq, lam_init):
    nq = seq // tq
    hw = 2 * HEAD_DIM
    qkv3 = qkv.reshape(batch, seq, 3 * DIFF_W)
    return pl.pallas_call(
        functools.partial(_diff_kernel, lam_init=lam_init),
        grid=(DIFF_HEADS, batch, nq, nq),
        in_specs=[pl.BlockSpec((4, HEAD_DIM), lambda h, b, i, j: (0, 0)),
                  pl.BlockSpec((1, tq, hw), lambda h, b, i, j: (b, i, h)),
                  pl.BlockSpec((1, tq, hw), lambda h, b, i, j: (b, jnp.minimum(i, j), DIFF_HEADS + h)),
                  pl.BlockSpec((1, tq, hw), lambda h, b, i, j: (b, jnp.minimum(i, j), 2 * DIFF_HEADS + h)),
                  pl.BlockSpec((1, nq, tq, tq), lambda h, b, i, j: (h, 0, 0, 0)),
                  pl.BlockSpec((1, hw), lambda h, b, i, j: (0, 0))],
        out_specs=pl.BlockSpec((1, tq, hw), lambda h, b, i, j: (b, i, h)),
        out_shape=jax.ShapeDtypeStruct((batch, seq, DIFF_W), BF16),
        scratch_shapes=[pltpu.VMEM((2, tq, 1), F32), pltpu.VMEM((2, tq, 1), F32), pltpu.VMEM((2, tq, hw), F32)],
        compiler_params=_params(("arbitrary", "arbitrary", "arbitrary", "arbitrary")),
        name="diff_attn",
    )(lam_vecs, qkv3, qkv3, qkv3, bias, subln_g).reshape(batch * seq, DIFF_W)


def _merge_kernel(o1, o2, o3, l1, l2, l3, ob_ref, gate_ref, x_ref, g1_ref, sh2_ref, sc2_ref, n2_ref,
                  wpa_ref, wpb_ref, wo_ref, wq_ref, sk_ref, x1_ref, h2_ref, sct_ref):
    la, lb, lc = l1[...], l2[...], l3[...]
    mx = jnp.maximum(jnp.maximum(la, lb), lc)
    ea, eb, ec = jnp.exp(la - mx), jnp.exp(lb - mx), jnp.exp(lc - mx)
    oa = (ea * o1[...].astype(F32) + eb * o2[...].astype(F32) + ec * o3[...].astype(F32)) / (ea + eb + ec)
    pa = jnp.dot(oa.astype(BF16), wpa_ref[...], preferred_element_type=F32)
    pb = jnp.dot(ob_ref[...], wpb_ref[...], preferred_element_type=F32)
    ga = jax.nn.sigmoid(gate_ref[:, 0:D_MODEL].astype(F32))
    gb = jax.nn.sigmoid(gate_ref[:, D_MODEL:2 * D_MODEL].astype(F32))
    merged = ga * pa + gb * pb
    y = jnp.dot(merged.astype(BF16), wo_ref[...], preferred_element_type=F32)
    x1 = x_ref[...] + g1_ref[0] * y
    x1_ref[...] = x1
    h2 = _modulated_norm(x1, n2_ref[...], sc2_ref[0], sh2_ref[0]).astype(BF16)
    h2_ref[...] = h2
    q = jnp.dot(h2, wq_ref[...], preferred_element_type=F32).astype(BF16)
    for hp in range(2 * PEER_HEADS):
        sct_ref[hp] = lax.dot_general(sk_ref[hp % 2], q[:, hp * PEER_DK:(hp + 1) * PEER_DK], NT_DIMS,
                                      preferred_element_type=F32)


def _merge_call(os_, ls_, ob, gates, x2, mod_l, n2, wpa, wpb, wo, wq, sk, seq, tm):
    t, d = x2.shape
    nper = seq // tm
    mod_spec = lambda k: pl.BlockSpec((1, 1, d), lambda i: (i // nper, 0, k))
    row = lambda w: pl.BlockSpec((tm, w), lambda i: (i, 0))
    full = lambda a: _resident(a.shape, lambda i: (0,) * a.ndim)
    return pl.pallas_call(
        _merge_kernel,
        grid=(t // tm,),
        in_specs=[row(GROUP_W)] * 6 + [row(DIFF_W), row(2 * D_MODEL), row(d), mod_spec(2), mod_spec(3), mod_spec(4),
                                       pl.BlockSpec((1, d), lambda i: (0, 0)),
                                       full(wpa), full(wpb), full(wo), full(wq), full(sk)],
        out_specs=[row(d), row(d), pl.BlockSpec((2 * PEER_HEADS, N_KEYS, tm), lambda i: (0, 0, i))],
        out_shape=[jax.ShapeDtypeStruct((t, d), F32), jax.ShapeDtypeStruct((t, d), BF16),
                   jax.ShapeDtypeStruct((2 * PEER_HEADS, N_KEYS, t), F32)],
        compiler_params=_params(("arbitrary",)),
        name="merge_proj",
    )(*os_, *ls_, ob, gates, x2, mod_l, mod_l, mod_l, n2, wpa, wpb, wo, wq, sk)


def _colmax(a):
    return jnp.max(a, axis=0, keepdims=True)


def _topk_kernel(sct_ref, n_ref, c_ref, r_ref, w_ref):
    tl = sct_ref.shape[-1]
    k = PEER_TOPK
    row16 = lax.broadcasted_iota(jnp.int32, (k, tl), 0)
    for h in range(PEER_HEADS):
        s0 = sct_ref[2 * h]
        s1 = sct_ref[2 * h + 1]

        def top_sorted(s, want_rank):
            work = s
            vals = jnp.full((k, tl), NEG_INF, F32)
            rank = jnp.full(s.shape, float(k), F32)
            for rr in range(k):
                v = _colmax(work)
                eq = work == v
                if want_rank:
                    rank = jnp.where(eq, float(rr), rank)
                work = jnp.where(eq, NEG_INF, work)
                vals = jnp.where(row16 == rr, v, vals)
            return vals, rank

        a, _ = top_sorted(s0, False)
        b, r1 = top_sorted(s1, True)
        cands = []
        for x in range(k):
            ny = k // (x + 1)
            cands.append(jnp.where(row16 < ny, a[x:x + 1] + b, NEG_INF))
        work = list(cands)
        c16 = None
        for rr in range(k):
            v = work[0]
            for w_ in work[1:]:
                v = jnp.maximum(v, w_)
            v = _colmax(v)
            work = [jnp.where(w_ == v, NEG_INF, w_) for w_ in work]
            c16 = v
        smax = a[0:1] + b[0:1]
        z = jnp.zeros((1, tl), F32)
        for cd in cands:
            z = z + jnp.sum(jnp.where(cd >= c16, jnp.exp(cd - smax), 0.0), axis=0, keepdims=True)
        n = jnp.zeros(s0.shape, F32)
        for y in range(k):
            n = n + jnp.where(s0 + b[y:y + 1] >= c16, 1.0, 0.0)
        n_ref[h] = n
        c_ref[h] = jnp.exp(s0 - a[0:1]) / z
        r_ref[h] = r1
        w_ref[h] = jnp.exp(s1 - b[0:1])


def _topk_call(sct, tl):
    hp, nk, t = sct.shape
    out = jax.ShapeDtypeStruct((PEER_HEADS, nk, t), F32)
    spec = pl.BlockSpec((PEER_HEADS, nk, tl), lambda i: (0, 0, i))
    return pl.pallas_call(
        _topk_kernel,
        grid=(t // tl,),
        in_specs=[pl.BlockSpec((hp, nk, tl), lambda i: (0, 0, i))],
        out_specs=[spec] * 4,
        out_shape=[out] * 4,
        compiler_params=_params(("arbitrary",)),
        name="peer_topk",
    )(sct)


def _gelu_tanh(x):
    return 0.5 * x * (1.0 + jnp.tanh(math.sqrt(2.0 / math.pi) * (x + 0.044715 * (x * x * x))))


def _peer_kernel(h2_ref, u_ref, vt_ref, n_ref, c_ref, r_ref, w_ref, x1_ref, g2_ref, o_ref, acc_s, ga_s, *, n_first):
    e = pl.program_id(1)
    tm = h2_ref.shape[0]

    @pl.when(e == 0)
    def _():
        acc_s[...] = jnp.zeros(acc_s.shape, F32)

    act = _gelu_tanh(lax.dot_general(u_ref[...], h2_ref[...], NT_DIMS, preferred_element_type=F32))
    for ii in range(n_first):
        for ch in range(tm // LANES):
            cols = slice(ch * LANES, (ch + 1) * LANES)
            gt = jnp.zeros((N_KEYS, LANES), F32)
            for h in range(PEER_HEADS):
                n_row = n_ref[h, ii:ii + 1, cols]
                c_row = c_ref[h, ii:ii + 1, cols]
                gt = gt + jnp.where(r_ref[h, :, cols] < n_row, w_ref[h, :, cols] * c_row, 0.0)
            ga_s[ii * N_KEYS:(ii + 1) * N_KEYS, cols] = (gt * act[ii * N_KEYS:(ii + 1) * N_KEYS, cols]).astype(BF16)
    acc_s[...] += jnp.dot(vt_ref[...], ga_s[...], preferred_element_type=F32)

    @pl.when(e == pl.num_programs(1) - 1)
    def _():
        o_ref[...] = x1_ref[...] + g2_ref[0] * acc_s[...].T


def _peer_call(h2, u_bf, vt_bf, n, c, r, w, x1, mod_l, seq, tm, te):
    t, d = x1.shape
    nper = seq // tm
    n_first = te // N_KEYS
    sel = pl.BlockSpec((PEER_HEADS, N_KEYS, tm), lambda i, e: (0, 0, i))
    first = pl.BlockSpec((PEER_HEADS, n_first, tm), lambda i, e: (0, e, i))
    return pl.pallas_call(
        functools.partial(_peer_kernel, n_first=n_first),
        grid=(t // tm, u_bf.shape[0] // te),
        in_specs=[pl.BlockSpec((tm, d), lambda i, e: (i, 0)),
                  pl.BlockSpec((te, d), lambda i, e: (e, 0)),
                  pl.BlockSpec((d, te), lambda i, e: (0, e)),
                  first, first, sel, sel,
                  pl.BlockSpec((tm, d), lambda i, e: (i, 0)),
                  pl.BlockSpec((1, 1, d), lambda i, e: (i // nper, 0, 5))],
        out_specs=pl.BlockSpec((tm, d), lambda i, e: (i, 0)),
        out_shape=jax.ShapeDtypeStruct((t, d), F32),
        scratch_shapes=[pltpu.VMEM((d, tm), F32), pltpu.VMEM((te, tm), BF16)],
        compiler_params=_params(("arbitrary", "arbitrary")),
        name="peer_dense",
    )(h2, u_bf, vt_bf, n, c, r, w, x1, mod_l)


def _final_kernel(x_ref, g_ref, o_ref):
    x = x_ref[...]
    ms = jnp.mean(x * x, axis=-1, keepdims=True)
    o_ref[...] = x * lax.rsqrt(ms + NORM_EPS) * g_ref[...]


def _final_call(x2, g, tm):
    t, d = x2.shape
    return pl.pallas_call(
        _final_kernel,
        grid=(t // tm,),
        in_specs=[pl.BlockSpec((tm, d), lambda i: (i, 0)), pl.BlockSpec((1, d), lambda i: (0, 0))],
        out_specs=pl.BlockSpec((tm, d), lambda i: (i, 0)),
        out_shape=jax.ShapeDtypeStruct((t, d), F32),
        compiler_params=_params(("arbitrary",)),
        name="final_norm",
    )(x2, g)


TOKEN_TILE = 512
DIFF_TILE = 512
TOPK_LANES = 128
EXPERT_TILE = 1024


def _mixer_layer(x2, mod_l, layer, lam_init, batch, seq, w_in_bf, norm1_g, dil_bias, diff_bias, lam_vecs, subln_g,
                 norm2_g, wpa, wpb, wo, wq, sk):
    p1, p2, p3, qkv, gates = _in_call(x2, mod_l, norm1_g, w_in_bf, seq, TOKEN_TILE)
    os_, ls_ = [], []
    for g, p in enumerate((p1, p2, p3)):
        o, lse = _dil_call(p, dil_bias[g], batch, seq, DIL_GROUPS[g][1])
        os_.append(o)
        ls_.append(lse)
    ob = _diff_call(qkv, diff_bias, lam_vecs, subln_g, batch, seq, DIFF_TILE, lam_init)
    return _merge_call(os_, ls_, ob, gates, x2, mod_l, norm2_g, wpa, wpb, wo, wq, sk, seq, TOKEN_TILE)


def _peer_layer(x1, h2, sct, mod_l, u_bf, vt_bf, seq):
    n, c, r, w = _topk_call(sct, TOPK_LANES)
    return _peer_call(h2, u_bf, vt_bf, n, c, r, w, x1, mod_l, seq, TOKEN_TILE, EXPERT_TILE)


def kernel(x, c, w_ada, b_ada, norm1_g, norm2_g, w_in, w_proj_a, w_proj_b, w_out, lam_q1, lam_k1, lam_q2, lam_k2,
           subln_g, rel_bias, peer_wq, peer_subkeys, peer_u, peer_v, final_g):
    batch, seq, d = x.shape
    depth = w_ada.shape[0]
    x2 = x.reshape(batch * seq, d)
    mods = _ada_call(c, w_ada, b_ada)
    dil_bias = [_dil_bias_table(rel_bias, g, DIL_GROUPS[g][1]) for g in range(N_DIL_GROUPS)]
    diff_bias = _diff_bias_table(rel_bias, DIFF_TILE, seq // DIFF_TILE)
    for l in range(depth):
        lam_init = 0.8 - 0.6 * math.exp(-0.3 * l)
        mod_l = mods[l].reshape(batch, 1, 6 * d)
        lam_vecs = jnp.stack([lam_q1[l], lam_k1[l], lam_q2[l], lam_k2[l]])
        x1, h2, sct = _mixer_layer(
            x2, mod_l, l, lam_init, batch, seq, w_in[l].astype(BF16), norm1_g[l].reshape(1, d), dil_bias, diff_bias,
            lam_vecs, subln_g[l].reshape(1, 2 * HEAD_DIM), norm2_g[l].reshape(1, d), w_proj_a[l].astype(BF16),
            w_proj_b[l].astype(BF16), w_out[l].astype(BF16), peer_wq[l].astype(BF16), peer_subkeys[l].astype(BF16))
        x2 = _peer_layer(x1, h2, sct, mod_l, peer_u[l].astype(BF16), peer_v[l].T.astype(BF16), seq)
    return _final_call(x2, final_g.reshape(1, d), TOKEN_TILE).reshape(batch, seq, d)
```

```python
import functools
import math

import numpy as np
import jax
import jax.numpy as jnp
from jax import lax
from jax.experimental import pallas as pl
from jax.experimental.pallas import tpu as pltpu

F32 = jnp.float32
BF16 = jnp.bfloat16

D_MODEL = 1024
HEAD_DIM = 64
DIL_GROUPS = ((128, 1), (512, 4), (2048, 16))
N_DIL_GROUPS = 3
HEADS_PER_GROUP = 8
GROUP_W = HEADS_PER_GROUP * HEAD_DIM
DIL_WIDTH = N_DIL_GROUPS * GROUP_W
DIFF_HEADS = 8
DIFF_W = DIFF_HEADS * 2 * HEAD_DIM
BLK = 128
NUM_BUCKETS = 32
MAX_DISTANCE = 2048
IN_WIDTH = 3 * DIL_WIDTH + 3 * DIFF_W + 2 * D_MODEL
PEER_HEADS = 8
PEER_DK = 128
N_KEYS = 128
PEER_TOPK = 16
NORM_EPS = 1e-6
SUBLN_EPS = 1e-5
NEG_INF = -1e30
Q_SCALE = HEAD_DIM ** -0.5

LANES = 128
VMEM_LIMIT = 56 * 1024 * 1024

NT_DIMS = (((1,), (1,)), ((), ()))


def _params(sem, vmem=VMEM_LIMIT):
    return pltpu.CompilerParams(dimension_semantics=sem, vmem_limit_bytes=vmem)


def _resident(shape, index_map):
    return pl.BlockSpec(shape, index_map, pipeline_mode=pl.Buffered(1))


def _ada_kernel(c_ref, w_ref, b_ref, o_ref):
    c = c_ref[...]
    cond = c * jax.nn.sigmoid(c)
    o_ref[0] = jnp.dot(cond, w_ref[0], preferred_element_type=F32) + b_ref[0]


def _ada_call(c, w_ada, b_ada):
    depth, d, n6 = w_ada.shape
    b = c.shape[0]
    nchunk = n6 // d
    return pl.pallas_call(
        _ada_kernel,
        grid=(depth, nchunk),
        in_specs=[pl.BlockSpec((b, d), lambda l, j: (0, 0)),
                  pl.BlockSpec((1, d, d), lambda l, j: (l, 0, j)),
                  pl.BlockSpec((1, 1, d), lambda l, j: (l, 0, j))],
        out_specs=pl.BlockSpec((1, b, d), lambda l, j: (l, 0, j)),
        out_shape=jax.ShapeDtypeStruct((depth, b, n6), F32),
        compiler_params=_params(("arbitrary", "arbitrary")),
        name="ada_mod",
    )(c, w_ada, b_ada.reshape(depth, 1, n6))


def _modulated_norm(x, g, sc, sh):
    ms = jnp.mean(x * x, axis=-1, keepdims=True)
    return (x * lax.rsqrt(ms + NORM_EPS) * g) * (1.0 + sc) + sh


def _in_kernel(x_ref, g_ref, sc_ref, sh_ref, w_ref, wvt_ref, p1_ref, p2_ref, p3_ref, qk_ref, vt_ref, gate_ref):
    hb = _modulated_norm(x_ref[...], g_ref[...], sc_ref[0], sh_ref[0]).astype(BF16)

    def mm(c0, scale=None):
        r = jnp.dot(hb, w_ref[:, c0:c0 + GROUP_W], preferred_element_type=F32)
        if scale is not None:
            r = r * scale
        return r.astype(BF16)

    for g, p_ref in enumerate((p1_ref, p2_ref, p3_ref)):
        p_ref[:, 0:GROUP_W] = mm(g * GROUP_W, Q_SCALE)
        p_ref[:, GROUP_W:2 * GROUP_W] = mm(DIL_WIDTH + g * GROUP_W)
        p_ref[:, 2 * GROUP_W:3 * GROUP_W] = mm(2 * DIL_WIDTH + g * GROUP_W)
    base = 3 * DIL_WIDTH
    for j in range(2 * DIFF_W // GROUP_W):
        qk_ref[:, j * GROUP_W:(j + 1) * GROUP_W] = mm(base + j * GROUP_W, Q_SCALE if j < DIFF_W // GROUP_W else None)
    vt_ref[...] = lax.dot_general(wvt_ref[...], hb, NT_DIMS, preferred_element_type=F32).astype(BF16)
    base += 3 * DIFF_W
    for j in range(2 * D_MODEL // GROUP_W):
        gate_ref[:, j * GROUP_W:(j + 1) * GROUP_W] = mm(base + j * GROUP_W)


def _in_call(x2, mod_l, g1, w_in_bf, w_vt_bf, seq, tm):
    t, d = x2.shape
    nper = seq // tm
    mod_spec = lambda k: pl.BlockSpec((1, 1, d), lambda i: (i // nper, 0, k))
    row = lambda w: pl.BlockSpec((tm, w), lambda i: (i, 0))
    return pl.pallas_call(
        _in_kernel,
        grid=(t // tm,),
        in_specs=[row(d), pl.BlockSpec((1, d), lambda i: (0, 0)), mod_spec(1), mod_spec(0),
                  _resident((d, IN_WIDTH), lambda i: (0, 0)), _resident((DIFF_W, d), lambda i: (0, 0))],
        out_specs=[row(3 * GROUP_W), row(3 * GROUP_W), row(3 * GROUP_W), row(2 * DIFF_W),
                   pl.BlockSpec((DIFF_W, tm), lambda i: (0, i)), row(2 * D_MODEL)],
        out_shape=[jax.ShapeDtypeStruct((t, 3 * GROUP_W), BF16)] * 3
        + [jax.ShapeDtypeStruct((t, 2 * DIFF_W), BF16), jax.ShapeDtypeStruct((DIFF_W, t), BF16),
           jax.ShapeDtypeStruct((t, 2 * D_MODEL), BF16)],
        compiler_params=_params(("arbitrary",)),
        name="in_proj",
    )(x2, g1, mod_l, mod_l, w_in_bf, w_vt_bf)


def _t5_bucket(dist):
    max_exact = NUM_BUCKETS // 2
    d = jnp.maximum(dist, 0)
    ratio = jnp.log(jnp.maximum(d, 1).astype(F32) / max_exact) / math.log(MAX_DISTANCE / max_exact)
    large = jnp.minimum(max_exact + (ratio * (NUM_BUCKETS - max_exact)).astype(jnp.int32), NUM_BUCKETS - 1)
    return jnp.where(d < max_exact, d, large)


def _toeplitz(u, n):
    heads, m = u.shape
    return jnp.tile(u, (1, n))[:, :n * (m - 1)].reshape(heads, n, m - 1)


def _dil_bias_table(rel_bias, g, r):
    n_back = DIL_GROUPS[g][0] // r
    assert n_back <= BLK
    delta = BLK - jnp.arange(3 * BLK)
    tab = rel_bias[:, g * HEADS_PER_GROUP:(g + 1) * HEADS_PER_GROUP]
    u = jnp.where(((delta >= 0) & (delta <= n_back))[None], tab[_t5_bucket(delta * r)].astype(F32).T, NEG_INF)
    return _toeplitz(u, BLK)[:, :, :2 * BLK]


def _diff_bias_table(rel_bias, tq, nd):
    dist = jnp.arange((nd + 1) * tq) - (tq - 1)
    tab = rel_bias[:, N_DIL_GROUPS * HEADS_PER_GROUP:]
    u = jnp.where((dist >= 0)[None], tab[_t5_bucket(dist)].astype(F32).T, NEG_INF)
    a = _toeplitz(u, tq)
    return jnp.stack([a[:, :, d * tq + tq - 1:d * tq + 2 * tq - 1] for d in range(nd)], axis=1)


def _dil_pair(q2, kc, vc, kp, vp, bias_ref, pair):
    lane = lax.broadcasted_iota(jnp.int32, (BLK, LANES), 1)
    low = lane < HEAD_DIM
    outs, lses = [], []
    for sub in range(2):
        head = 2 * pair + sub
        qm = jnp.where(low if sub == 0 else jnp.logical_not(low), q2, jnp.zeros_like(q2))
        sc = lax.dot_general(qm, kc, NT_DIMS, preferred_element_type=F32) + bias_ref[head, :, BLK:2 * BLK]
        mx = jnp.max(sc, axis=-1, keepdims=True)
        if kp is not None:
            sp = lax.dot_general(qm, kp, NT_DIMS, preferred_element_type=F32) + bias_ref[head, :, 0:BLK]
            mx = jnp.maximum(mx, jnp.max(sp, axis=-1, keepdims=True))
        pc = jnp.exp(sc - mx)
        den = jnp.sum(pc, axis=-1, keepdims=True)
        acc = jnp.dot(pc.astype(BF16), vc, preferred_element_type=F32)
        if kp is not None:
            pp = jnp.exp(sp - mx)
            den = den + jnp.sum(pp, axis=-1, keepdims=True)
            acc = acc + jnp.dot(pp.astype(BF16), vp, preferred_element_type=F32)
        outs.append(acc / den)
        lses.append(mx + jnp.log(den))
    o = jnp.where(low, outs[0], outs[1])
    lse = jnp.where(low, lses[0], lses[1])
    return o, lse


def _dil_kernel(p_ref, bias_ref, o_ref, lse_ref, *, r, nb):
    pw = 3 * GROUP_W
    for m in range(r):
        for pair in range(HEADS_PER_GROUP // 2):
            qc = m * pw + pair * LANES
            kc_ = qc + GROUP_W
            vc_ = qc + 2 * GROUP_W
            oc = m * GROUP_W + pair * LANES

            def tile(row, prev_row):
                q2 = p_ref[0, pl.ds(row, BLK), qc:qc + LANES]
                kc = p_ref[0, pl.ds(row, BLK), kc_:kc_ + LANES]
                vc = p_ref[0, pl.ds(row, BLK), vc_:vc_ + LANES]
                if prev_row is None:
                    kp = vp = None
                else:
                    kp = p_ref[0, pl.ds(prev_row, BLK), kc_:kc_ + LANES]
                    vp = p_ref[0, pl.ds(prev_row, BLK), vc_:vc_ + LANES]
                o, lse = _dil_pair(q2, kc, vc, kp, vp, bias_ref, pair)
                o_ref[0, pl.ds(row, BLK), oc:oc + LANES] = o.astype(BF16)
                lse_ref[0, pl.ds(row, BLK), oc:oc + LANES] = lse

            tile(0, None)
            if nb > 1:
                def body(n, carry):
                    row = pl.multiple_of(n * BLK, BLK)
                    tile(row, pl.multiple_of(row - BLK, BLK))
                    return carry
                lax.fori_loop(1, nb, body, 0)


def _dil_call(p, bias, batch, seq, r):
    length = seq // r
    nb = length // BLK
    pv = p.reshape(batch, length, r * 3 * GROUP_W)
    o, lse = pl.pallas_call(
        functools.partial(_dil_kernel, r=r, nb=nb),
        grid=(batch,),
        in_specs=[pl.BlockSpec((1, length, r * 3 * GROUP_W), lambda b: (b, 0, 0)),
                  _resident(bias.shape, lambda b: (0, 0, 0))],
        out_specs=[pl.BlockSpec((1, length, r * GROUP_W), lambda b: (b, 0, 0))] * 2,
        out_shape=[jax.ShapeDtypeStruct((batch, length, r * GROUP_W), BF16),
                   jax.ShapeDtypeStruct((batch, length, r * GROUP_W), F32)],
        compiler_params=_params(("arbitrary",)),
        name=f"dilated_r{r}",
    )(pv, bias)
    return o.reshape(batch * seq, GROUP_W), lse.reshape(batch * seq, GROUP_W)


def _diff_kernel(qtab_ref, ktab_ref, lam_ref, q_ref, k_ref, vt_ref, bias_ref, sg_ref, o_ref, m_s, l_s, acc_s, *,
                 lam_init):
    step = pl.program_id(2)
    qi = qtab_ref[step]
    ki = ktab_ref[step]

    @pl.when(ki == 0)
    def _():
        m_s[...] = jnp.full(m_s.shape, NEG_INF, F32)
        l_s[...] = jnp.zeros(l_s.shape, F32)
        acc_s[...] = jnp.zeros(acc_s.shape, F32)

    q = q_ref[0]
    k = k_ref[0]
    vt = vt_ref[...]
    bias = bias_ref[0, qi - ki]
    lane = lax.broadcasted_iota(jnp.int32, q.shape, 1)
    for t in range(2):
        qm = jnp.where((lane < HEAD_DIM) if t == 0 else (lane >= HEAD_DIM), q, jnp.zeros_like(q))
        s = lax.dot_general(k, qm, NT_DIMS, preferred_element_type=F32) + bias
        m_prev = m_s[t]
        m_new = jnp.maximum(m_prev, jnp.max(s, axis=0, keepdims=True))
        alpha = jnp.exp(m_prev - m_new)
        p = jnp.exp(s - m_new)
        l_s[t] = alpha * l_s[t] + jnp.sum(p, axis=0, keepdims=True)
        acc_s[t] = alpha * acc_s[t] + jnp.dot(vt, p.astype(BF16), preferred_element_type=F32)
        m_s[t] = m_new

    @pl.when(ki == qi)
    def _():
        lv = lam_ref[...]
        lam = (jnp.exp(jnp.sum(lv[0:1] * lv[1:2], axis=-1, keepdims=True))
               - jnp.exp(jnp.sum(lv[2:3] * lv[3:4], axis=-1, keepdims=True)) + lam_init)
        o = acc_s[0] / l_s[0] - lam * (acc_s[1] / l_s[1])
        ms = jnp.mean(o * o, axis=0, keepdims=True)
        o = (o * lax.rsqrt(ms + SUBLN_EPS) * sg_ref[...]) * (1.0 - lam_init)
        o_ref[0] = o.T.astype(BF16)


def _diff_call(qk, vt, bias, lam_vecs, subln_g, batch, seq, tq, lam_init):
    nq = seq // tq
    hw = 2 * HEAD_DIM
    qk3 = qk.reshape(batch, seq, 2 * DIFF_W)
    pairs = [(i, j) for i in range(nq) for j in range(i + 1)]
    qtab = jnp.asarray([p[0] for p in pairs], jnp.int32)
    ktab = jnp.asarray([p[1] for p in pairs], jnp.int32)
    grid_spec = pltpu.PrefetchScalarGridSpec(
        num_scalar_prefetch=2,
        grid=(DIFF_HEADS, batch, len(pairs)),
        in_specs=[pl.BlockSpec((4, HEAD_DIM), lambda h, b, s, qt, kt: (0, 0)),
                  pl.BlockSpec((1, tq, hw), lambda h, b, s, qt, kt: (b, qt[s], h)),
                  pl.BlockSpec((1, tq, hw), lambda h, b, s, qt, kt: (b, kt[s], DIFF_HEADS + h)),
                  pl.BlockSpec((hw, tq), lambda h, b, s, qt, kt: (h, b * nq + kt[s])),
                  pl.BlockSpec((1, nq, tq, tq), lambda h, b, s, qt, kt: (h, 0, 0, 0)),
                  pl.BlockSpec((hw, 1), lambda h, b, s, qt, kt: (0, 0))],
        out_specs=pl.BlockSpec((1, tq, hw), lambda h, b, s, qt, kt: (b, qt[s], h)),
        scratch_shapes=[pltpu.VMEM((2, 1, tq), F32), pltpu.VMEM((2, 1, tq), F32), pltpu.VMEM((2, hw, tq), F32)])
    return pl.pallas_call(
        functools.partial(_diff_kernel, lam_init=lam_init),
        grid_spec=grid_spec,
        out_shape=jax.ShapeDtypeStruct((batch, seq, DIFF_W), BF16),
        compiler_params=_params(("arbitrary", "arbitrary", "arbitrary")),
        name="diff_attn",
    )(qtab, ktab, lam_vecs, qk3, qk3, vt, bias, subln_g).reshape(batch * seq, DIFF_W)


def _merge_kernel(o1, o2, o3, l1, l2, l3, ob_ref, gate_ref, x_ref, g1_ref, sh2_ref, sc2_ref, n2_ref,
                  wpa_ref, wpb_ref, wo_ref, wq_ref, sk_ref, x1_ref, h2_ref, sct_ref):
    la, lb, lc = l1[...], l2[...], l3[...]
    mx = jnp.maximum(jnp.maximum(la, lb), lc)
    ea, eb, ec = jnp.exp(la - mx), jnp.exp(lb - mx), jnp.exp(lc - mx)
    oa = (ea * o1[...].astype(F32) + eb * o2[...].astype(F32) + ec * o3[...].astype(F32)) / (ea + eb + ec)
    pa = jnp.dot(oa.astype(BF16), wpa_ref[...], preferred_element_type=F32)
    pb = jnp.dot(ob_ref[...], wpb_ref[...], preferred_element_type=F32)
    ga = jax.nn.sigmoid(gate_ref[:, 0:D_MODEL].astype(F32))
    gb = jax.nn.sigmoid(gate_ref[:, D_MODEL:2 * D_MODEL].astype(F32))
    merged = ga * pa + gb * pb
    y = jnp.dot(merged.astype(BF16), wo_ref[...], preferred_element_type=F32)
    x1 = x_ref[...] + g1_ref[0] * y
    x1_ref[...] = x1
    h2 = _modulated_norm(x1, n2_ref[...], sc2_ref[0], sh2_ref[0]).astype(BF16)
    h2_ref[...] = h2
    q = jnp.dot(h2, wq_ref[...], preferred_element_type=F32).astype(BF16)
    for hp in range(2 * PEER_HEADS):
        sct_ref[hp] = lax.dot_general(sk_ref[hp % 2], q[:, hp * PEER_DK:(hp + 1) * PEER_DK], NT_DIMS,
                                      preferred_element_type=F32)


def _merge_call(os_, ls_, ob, gates, x2, mod_l, n2, wpa, wpb, wo, wq, sk, seq, tm):
    t, d = x2.shape
    nper = seq // tm
    mod_spec = lambda k: pl.BlockSpec((1, 1, d), lambda i: (i // nper, 0, k))
    row = lambda w: pl.BlockSpec((tm, w), lambda i: (i, 0))
    full = lambda a: _resident(a.shape, lambda i: (0,) * a.ndim)
    return pl.pallas_call(
        _merge_kernel,
        grid=(t // tm,),
        in_specs=[row(GROUP_W)] * 6 + [row(DIFF_W), row(2 * D_MODEL), row(d), mod_spec(2), mod_spec(3), mod_spec(4),
                                       pl.BlockSpec((1, d), lambda i: (0, 0)),
                                       full(wpa), full(wpb), full(wo), full(wq), full(sk)],
        out_specs=[row(d), row(d), pl.BlockSpec((2 * PEER_HEADS, N_KEYS, tm), lambda i: (0, 0, i))],
        out_shape=[jax.ShapeDtypeStruct((t, d), F32), jax.ShapeDtypeStruct((t, d), BF16),
                   jax.ShapeDtypeStruct((2 * PEER_HEADS, N_KEYS, t), F32)],
        compiler_params=_params(("arbitrary",)),
        name="merge_proj",
    )(*os_, *ls_, ob, gates, x2, mod_l, mod_l, mod_l, n2, wpa, wpb, wo, wq, sk)


def _colmax(a):
    return jnp.max(a, axis=0, keepdims=True)


def _bf16_pair_words(x):
    hi = pltpu.bitcast(x.astype(BF16).astype(F32), jnp.uint32)
    return hi | (hi >> 16)


def _bf16_half_words(x):
    n = x.shape[0] // 2
    bits = pltpu.bitcast(x.astype(BF16).astype(F32), jnp.uint32)
    return (bits[0:n] >> 16) | (bits[n:2 * n] & jnp.uint32(0xFFFF0000))


def _second_key_order(table):
    e, d = table.shape
    half = N_KEYS // 2
    return table.reshape(e // N_KEYS, 2, half, d).transpose(0, 2, 1, 3).reshape(e, d)


def _packed_rows(word_row, rows):
    v = pltpu.bitcast(jnp.broadcast_to(word_row, (8, word_row.shape[-1])), BF16)
    return jnp.tile(v, (rows // v.shape[0], 1))


def _topk_kernel(sct_ref, n_ref, c_ref, r_ref, w_ref):
    tl = sct_ref.shape[-1]
    k = PEER_TOPK
    row16 = lax.broadcasted_iota(jnp.int32, (k, tl), 0)
    for h in range(PEER_HEADS):
        s0 = sct_ref[2 * h]
        s1 = sct_ref[2 * h + 1]

        def top_sorted(s, want_rank):
            work = s
            vals = jnp.full((k, tl), NEG_INF, F32)
            rank = jnp.full(s.shape, float(k), F32)
            for rr in range(k):
                v = _colmax(work)
                eq = work == v
                if want_rank:
                    rank = jnp.where(eq, float(rr), rank)
                work = jnp.where(eq, NEG_INF, work)
                vals = jnp.where(row16 == rr, v, vals)
            return vals, rank

        a, _ = top_sorted(s0, False)
        b, r1 = top_sorted(s1, True)
        cands = []
        for x in range(k):
            ny = k // (x + 1)
            cands.append(jnp.where(row16 < ny, a[x:x + 1] + b, NEG_INF))
        work = list(cands)
        c16 = None
        for rr in range(k):
            v = work[0]
            for w_ in work[1:]:
                v = jnp.maximum(v, w_)
            v = _colmax(v)
            work = [jnp.where(w_ == v, NEG_INF, w_) for w_ in work]
            c16 = v
        smax = a[0:1] + b[0:1]
        z = jnp.zeros((1, tl), F32)
        for cd in cands:
            z = z + jnp.sum(jnp.where(cd >= c16, jnp.exp(cd - smax), 0.0), axis=0, keepdims=True)
        n = jnp.zeros(s0.shape, F32)
        for y in range(k):
            n = n + jnp.where(s0 + b[y:y + 1] >= c16, 1.0, 0.0)
        n_ref[h] = _bf16_pair_words(n)
        c_ref[h] = _bf16_pair_words(jnp.exp(s0 - a[0:1]) / z)
        r_ref[h] = _bf16_half_words(r1)
        w_ref[h] = _bf16_half_words(jnp.exp(s1 - b[0:1]))


def _topk_call(sct, tl):
    hp, nk, t = sct.shape
    spec = lambda rows: pl.BlockSpec((PEER_HEADS, rows, tl), lambda i: (0, 0, i))
    return pl.pallas_call(
        _topk_kernel,
        grid=(t // tl,),
        in_specs=[pl.BlockSpec((hp, nk, tl), lambda i: (0, 0, i))],
        out_specs=[spec(nk), spec(nk), spec(nk // 2), spec(nk // 2)],
        out_shape=[jax.ShapeDtypeStruct((PEER_HEADS, nk, t), jnp.uint32)] * 2
        + [jax.ShapeDtypeStruct((PEER_HEADS, nk // 2, t), jnp.uint32)] * 2,
        compiler_params=_params(("arbitrary",)),
        name="peer_topk",
    )(sct)


def _gelu_tanh(x):
    return 0.5 * x * (1.0 + jnp.tanh(math.sqrt(2.0 / math.pi) * (x + 0.044715 * (x * x * x))))


PEER_COLS = 512


def _peer_kernel(h2_ref, u_ref, vt_ref, n_ref, c_ref, r_ref, w_ref, x1_ref, g2_ref, o_ref, acc_s, ga_s, *, n_first):
    e = pl.program_id(1)
    tm = h2_ref.shape[0]

    @pl.when(e == 0)
    def _():
        acc_s[...] = jnp.zeros(acc_s.shape, F32)

    for cb in range(tm // PEER_COLS):
        blk = slice(cb * PEER_COLS, (cb + 1) * PEER_COLS)
        act = _gelu_tanh(lax.dot_general(u_ref[...], h2_ref[blk, :], NT_DIMS, preferred_element_type=F32))
        for ii in range(n_first):
            rows = slice(ii * N_KEYS, (ii + 1) * N_KEYS)
            for ch in range(PEER_COLS // LANES):
                cols = slice(cb * PEER_COLS + ch * LANES, cb * PEER_COLS + (ch + 1) * LANES)
                gt = jnp.zeros((N_KEYS, LANES), BF16)
                for h in range(PEER_HEADS):
                    n_row = _packed_rows(n_ref[h, ii:ii + 1, cols], N_KEYS)
                    c_row = _packed_rows(c_ref[h, ii:ii + 1, cols], N_KEYS)
                    r_t = pltpu.bitcast(r_ref[h, :, cols], BF16)
                    w_t = pltpu.bitcast(w_ref[h, :, cols], BF16)
                    gt = gt + jnp.where(r_t < n_row, w_t * c_row, jnp.zeros_like(gt))
                ga_s[rows, cols] = gt * act[rows, ch * LANES:(ch + 1) * LANES].astype(BF16)
        acc_s[:, blk] += jnp.dot(vt_ref[...], ga_s[:, blk], preferred_element_type=F32)

    @pl.when(e == pl.num_programs(1) - 1)
    def _():
        o_ref[...] = x1_ref[...] + g2_ref[0] * acc_s[...].T


def _peer_call(h2, u_bf, vt_bf, n, c, r, w, x1, mod_l, seq, tm, te):
    t, d = x1.shape
    nper = seq // tm
    n_first = te // N_KEYS
    sel = pl.BlockSpec((PEER_HEADS, N_KEYS // 2, tm), lambda i, e: (0, 0, i))
    first = pl.BlockSpec((PEER_HEADS, n_first, tm), lambda i, e: (0, e, i))
    return pl.pallas_call(
        functools.partial(_peer_kernel, n_first=n_first),
        grid=(t // tm, u_bf.shape[0] // te),
        in_specs=[pl.BlockSpec((tm, d), lambda i, e: (i, 0)),
                  pl.BlockSpec((te, d), lambda i, e: (e, 0)),
                  pl.BlockSpec((d, te), lambda i, e: (0, e)),
                  first, first, sel, sel,
                  pl.BlockSpec((tm, d), lambda i, e: (i, 0)),
                  pl.BlockSpec((1, 1, d), lambda i, e: (i // nper, 0, 5))],
        out_specs=pl.BlockSpec((tm, d), lambda i, e: (i, 0)),
        out_shape=jax.ShapeDtypeStruct((t, d), F32),
        scratch_shapes=[pltpu.VMEM((d, tm), F32), pltpu.VMEM((te, tm), BF16)],
        compiler_params=_params(("arbitrary", "arbitrary")),
        name="peer_dense",
    )(h2, u_bf, vt_bf, n, c, r, w, x1, mod_l)


def _final_kernel(x_ref, g_ref, o_ref):
    x = x_ref[...]
    ms = jnp.mean(x * x, axis=-1, keepdims=True)
    o_ref[...] = x * lax.rsqrt(ms + NORM_EPS) * g_ref[...]


def _final_call(x2, g, tm):
    t, d = x2.shape
    return pl.pallas_call(
        _final_kernel,
        grid=(t // tm,),
        in_specs=[pl.BlockSpec((tm, d), lambda i: (i, 0)), pl.BlockSpec((1, d), lambda i: (0, 0))],
        out_specs=pl.BlockSpec((tm, d), lambda i: (i, 0)),
        out_shape=jax.ShapeDtypeStruct((t, d), F32),
        compiler_params=_params(("arbitrary",)),
        name="final_norm",
    )(x2, g)


TOKEN_TILE = 512
DIFF_TILE = 512
TOPK_LANES = 128
EXPERT_TILE = 1024


def _mixer_layer(x2, mod_l, layer, lam_init, batch, seq, w_in_bf, w_vt_bf, norm1_g, dil_bias, diff_bias, lam_vecs,
                 subln_g, norm2_g, wpa, wpb, wo, wq, sk):
    p1, p2, p3, qk, vt, gates = _in_call(x2, mod_l, norm1_g, w_in_bf, w_vt_bf, seq, TOKEN_TILE)
    os_, ls_ = [], []
    for g, p in enumerate((p1, p2, p3)):
        o, lse = _dil_call(p, dil_bias[g], batch, seq, DIL_GROUPS[g][1])
        os_.append(o)
        ls_.append(lse)
    ob = _diff_call(qk, vt, diff_bias, lam_vecs, subln_g, batch, seq, DIFF_TILE, lam_init)
    return _merge_call(os_, ls_, ob, gates, x2, mod_l, norm2_g, wpa, wpb, wo, wq, sk, seq, TOKEN_TILE)


def _peer_layer(x1, h2, sct, mod_l, u_bf, vt_bf, seq):
    n, c, r, w = _topk_call(sct, TOPK_LANES)
    return _peer_call(h2, u_bf, vt_bf, n, c, r, w, x1, mod_l, seq, TOKEN_TILE, EXPERT_TILE)


def kernel(x, c, w_ada, b_ada, norm1_g, norm2_g, w_in, w_proj_a, w_proj_b, w_out, lam_q1, lam_k1, lam_q2, lam_k2,
           subln_g, rel_bias, peer_wq, peer_subkeys, peer_u, peer_v, final_g):
    batch, seq, d = x.shape
    depth = w_ada.shape[0]
    x2 = x.reshape(batch * seq, d)
    mods = _ada_call(c, w_ada, b_ada)
    dil_bias = [_dil_bias_table(rel_bias, g, DIL_GROUPS[g][1]) for g in range(N_DIL_GROUPS)]
    diff_bias = _diff_bias_table(rel_bias, DIFF_TILE, seq // DIFF_TILE)
    for l in range(depth):
        lam_init = 0.8 - 0.6 * math.exp(-0.3 * l)
        mod_l = mods[l].reshape(batch, 1, 6 * d)
        lam_vecs = jnp.stack([lam_q1[l], lam_k1[l], lam_q2[l], lam_k2[l]])
        v_cols = slice(3 * DIL_WIDTH + 2 * DIFF_W, 3 * DIL_WIDTH + 3 * DIFF_W)
        x1, h2, sct = _mixer_layer(
            x2, mod_l, l, lam_init, batch, seq, w_in[l].astype(BF16), w_in[l][:, v_cols].T.astype(BF16),
            norm1_g[l].reshape(1, d), dil_bias, diff_bias,
            lam_vecs, subln_g[l].reshape(2 * HEAD_DIM, 1), norm2_g[l].reshape(1, d), w_proj_a[l].astype(BF16),
            w_proj_b[l].astype(BF16), w_out[l].astype(BF16), peer_wq[l].astype(BF16), peer_subkeys[l].astype(BF16))
        x2 = _peer_layer(x1, h2, sct, mod_l, _second_key_order(peer_u[l]).astype(BF16),
                         _second_key_order(peer_v[l]).T.astype(BF16), seq)
    return _final_call(x2, final_g.reshape(1, d), TOKEN_TILE).reshape(batch, seq, d)
```

```python
import functools
import math

import numpy as np
import jax
import jax.numpy as jnp
from jax import lax
from jax.experimental import pallas as pl
from jax.experimental.pallas import tpu as pltpu

F32 = jnp.float32
BF16 = jnp.bfloat16

D_MODEL = 1024
HEAD_DIM = 64
DIL_GROUPS = ((128, 1), (512, 4), (2048, 16))
N_DIL_GROUPS = 3
HEADS_PER_GROUP = 8
GROUP_W = HEADS_PER_GROUP * HEAD_DIM
DIL_WIDTH = N_DIL_GROUPS * GROUP_W
DIFF_HEADS = 8
DIFF_W = DIFF_HEADS * 2 * HEAD_DIM
BLK = 128
NUM_BUCKETS = 32
MAX_DISTANCE = 2048
IN_WIDTH = 3 * DIL_WIDTH + 3 * DIFF_W + 2 * D_MODEL
PEER_HEADS = 8
PEER_DK = 128
N_KEYS = 128
PEER_TOPK = 16
NORM_EPS = 1e-6
SUBLN_EPS = 1e-5
NEG_INF = -1e30
Q_SCALE = HEAD_DIM ** -0.5

LANES = 128
VMEM_LIMIT = 56 * 1024 * 1024

NT_DIMS = (((1,), (1,)), ((), ()))


def _params(sem, vmem=VMEM_LIMIT):
    return pltpu.CompilerParams(dimension_semantics=sem, vmem_limit_bytes=vmem)


def _resident(shape, index_map):
    return pl.BlockSpec(shape, index_map, pipeline_mode=pl.Buffered(1))


def _ada_kernel(c_ref, w_ref, b_ref, o_ref):
    c = c_ref[...]
    cond = c * jax.nn.sigmoid(c)
    o_ref[0] = jnp.dot(cond, w_ref[0], preferred_element_type=F32) + b_ref[0]


def _ada_call(c, w_ada, b_ada):
    depth, d, n6 = w_ada.shape
    b = c.shape[0]
    nchunk = n6 // d
    return pl.pallas_call(
        _ada_kernel,
        grid=(depth, nchunk),
        in_specs=[pl.BlockSpec((b, d), lambda l, j: (0, 0)),
                  pl.BlockSpec((1, d, d), lambda l, j: (l, 0, j)),
                  pl.BlockSpec((1, 1, d), lambda l, j: (l, 0, j))],
        out_specs=pl.BlockSpec((1, b, d), lambda l, j: (l, 0, j)),
        out_shape=jax.ShapeDtypeStruct((depth, b, n6), F32),
        compiler_params=_params(("arbitrary", "arbitrary")),
        name="ada_mod",
    )(c, w_ada, b_ada.reshape(depth, 1, n6))


def _modulated_norm(x, g, sc, sh):
    ms = jnp.mean(x * x, axis=-1, keepdims=True)
    return (x * lax.rsqrt(ms + NORM_EPS) * g) * (1.0 + sc) + sh


def _in_kernel(x_ref, g_ref, sc_ref, sh_ref, w_ref, wvt_ref, p1_ref, p2_ref, p3_ref, qk_ref, vt_ref, gate_ref,
               stage_s):
    hb = _modulated_norm(x_ref[...], g_ref[...], sc_ref[0], sh_ref[0]).astype(BF16)
    tm = hb.shape[0]

    def mm32(c0, scale=None):
        r = jnp.dot(hb, w_ref[:, c0:c0 + GROUP_W], preferred_element_type=F32)
        return r if scale is None else r * scale

    def mm(c0, scale=None):
        return mm32(c0, scale).astype(BF16)

    p1_ref[:, 0:GROUP_W] = mm(0, Q_SCALE)
    p1_ref[:, GROUP_W:2 * GROUP_W] = mm(DIL_WIDTH)
    p1_ref[:, 2 * GROUP_W:3 * GROUP_W] = mm(2 * DIL_WIDTH)
    for g, p_ref in ((1, p2_ref), (2, p3_ref)):
        r = DIL_GROUPS[g][1]
        for piece in range(3):
            res = mm32(piece * DIL_WIDTH + g * GROUP_W, Q_SCALE if piece == 0 else None)
            for k in range(GROUP_W // LANES):
                stage_s[k] = res[:, k * LANES:(k + 1) * LANES]
            for m in range(r):
                for k in range(GROUP_W // LANES):
                    c0 = m * 3 * GROUP_W + piece * GROUP_W + k * LANES
                    p_ref[0, :, c0:c0 + LANES] = stage_s[k, pl.ds(m, tm // r, stride=r), :].astype(BF16)
    base = 3 * DIL_WIDTH
    for j in range(2 * DIFF_W // GROUP_W):
        qk_ref[:, j * GROUP_W:(j + 1) * GROUP_W] = mm(base + j * GROUP_W, Q_SCALE if j < DIFF_W // GROUP_W else None)
    vt_ref[...] = lax.dot_general(wvt_ref[...], hb, NT_DIMS, preferred_element_type=F32).astype(BF16)
    base += 3 * DIFF_W
    for j in range(2 * D_MODEL // GROUP_W):
        gate_ref[:, j * GROUP_W:(j + 1) * GROUP_W] = mm(base + j * GROUP_W)


def _in_call(x2, mod_l, g1, w_in_bf, w_vt_bf, seq, tm):
    t, d = x2.shape
    nper = seq // tm
    mod_spec = lambda k: pl.BlockSpec((1, 1, d), lambda i: (i // nper, 0, k))
    row = lambda w: pl.BlockSpec((tm, w), lambda i: (i, 0))
    batch = t // seq
    pw = 3 * GROUP_W
    dil_spec = lambda r, w: pl.BlockSpec((1, tm // r, r * w), lambda i: (i // nper, i % nper, 0))
    dil_shape = lambda r, w, dt: jax.ShapeDtypeStruct((batch, seq // r, r * w), dt)
    r2, r3 = DIL_GROUPS[1][1], DIL_GROUPS[2][1]
    return pl.pallas_call(
        _in_kernel,
        grid=(t // tm,),
        in_specs=[row(d), pl.BlockSpec((1, d), lambda i: (0, 0)), mod_spec(1), mod_spec(0),
                  _resident((d, IN_WIDTH), lambda i: (0, 0)), _resident((DIFF_W, d), lambda i: (0, 0))],
        out_specs=[row(pw), dil_spec(r2, pw), dil_spec(r3, pw), row(2 * DIFF_W),
                   pl.BlockSpec((DIFF_W, tm), lambda i: (0, i)), row(2 * D_MODEL)],
        out_shape=[jax.ShapeDtypeStruct((t, pw), BF16), dil_shape(r2, pw, BF16), dil_shape(r3, pw, BF16),
                   jax.ShapeDtypeStruct((t, 2 * DIFF_W), BF16), jax.ShapeDtypeStruct((DIFF_W, t), BF16),
                   jax.ShapeDtypeStruct((t, 2 * D_MODEL), BF16)],
        scratch_shapes=[pltpu.VMEM((GROUP_W // LANES, tm, LANES), F32)],
        compiler_params=_params(("arbitrary",)),
        name="in_proj",
    )(x2, g1, mod_l, mod_l, w_in_bf, w_vt_bf)


def _t5_bucket(dist):
    max_exact = NUM_BUCKETS // 2
    d = jnp.maximum(dist, 0)
    ratio = jnp.log(jnp.maximum(d, 1).astype(F32) / max_exact) / math.log(MAX_DISTANCE / max_exact)
    large = jnp.minimum(max_exact + (ratio * (NUM_BUCKETS - max_exact)).astype(jnp.int32), NUM_BUCKETS - 1)
    return jnp.where(d < max_exact, d, large)


def _toeplitz(u, n):
    heads, m = u.shape
    return jnp.tile(u, (1, n))[:, :n * (m - 1)].reshape(heads, n, m - 1)


def _dil_bias_table(rel_bias, g, r):
    n_back = DIL_GROUPS[g][0] // r
    assert n_back <= BLK
    delta = BLK - jnp.arange(3 * BLK)
    tab = rel_bias[:, g * HEADS_PER_GROUP:(g + 1) * HEADS_PER_GROUP]
    u = jnp.where(((delta >= 0) & (delta <= n_back))[None], tab[_t5_bucket(delta * r)].astype(F32).T, NEG_INF)
    return _toeplitz(u, BLK)[:, :, :2 * BLK]


def _diff_bias_table(rel_bias, tq, nd):
    dist = jnp.arange((nd + 1) * tq) - (tq - 1)
    tab = rel_bias[:, N_DIL_GROUPS * HEADS_PER_GROUP:]
    u = jnp.where((dist >= 0)[None], tab[_t5_bucket(dist)].astype(F32).T, NEG_INF)
    a = _toeplitz(u, tq)
    return jnp.stack([a[:, :, d * tq + tq - 1:d * tq + 2 * tq - 1] for d in range(nd)], axis=1)


def _dil_pair(q2, kc, vc, kp, vp, bias_ref, pair):
    lane = lax.broadcasted_iota(jnp.int32, (BLK, LANES), 1)
    low = lane < HEAD_DIM
    outs, lses = [], []
    for sub in range(2):
        head = 2 * pair + sub
        qm = jnp.where(low if sub == 0 else jnp.logical_not(low), q2, jnp.zeros_like(q2))
        sc = lax.dot_general(qm, kc, NT_DIMS, preferred_element_type=F32) + bias_ref[head, :, BLK:2 * BLK]
        mx = jnp.max(sc, axis=-1, keepdims=True)
        if kp is not None:
            sp = lax.dot_general(qm, kp, NT_DIMS, preferred_element_type=F32) + bias_ref[head, :, 0:BLK]
            mx = jnp.maximum(mx, jnp.max(sp, axis=-1, keepdims=True))
        pc = jnp.exp(sc - mx)
        den = jnp.sum(pc, axis=-1, keepdims=True)
        acc = jnp.dot(pc.astype(BF16), vc, preferred_element_type=F32)
        if kp is not None:
            pp = jnp.exp(sp - mx)
            den = den + jnp.sum(pp, axis=-1, keepdims=True)
            acc = acc + jnp.dot(pp.astype(BF16), vp, preferred_element_type=F32)
        outs.append(acc / den)
        lses.append(mx + jnp.log(den))
    o = jnp.where(low, outs[0], outs[1])
    lse = jnp.where(low, lses[0], lses[1])
    return o, lse


def _dil_kernel(p_ref, bias_ref, o_ref, lse_ref, *, r, nb):
    pw = 3 * GROUP_W

    def tile(m, pair, row, prev_row):
        qc = m * pw + pair * LANES
        kc_ = qc + GROUP_W
        vc_ = qc + 2 * GROUP_W
        oc = m * GROUP_W + pair * LANES
        q2 = p_ref[0, pl.ds(row, BLK), qc:qc + LANES]
        kc = p_ref[0, pl.ds(row, BLK), kc_:kc_ + LANES]
        vc = p_ref[0, pl.ds(row, BLK), vc_:vc_ + LANES]
        if prev_row is None:
            kp = vp = None
        else:
            kp = p_ref[0, pl.ds(prev_row, BLK), kc_:kc_ + LANES]
            vp = p_ref[0, pl.ds(prev_row, BLK), vc_:vc_ + LANES]
        o, lse = _dil_pair(q2, kc, vc, kp, vp, bias_ref, pair)
        o_ref[0, pl.ds(row, BLK), oc:oc + LANES] = o.astype(BF16)
        lse_ref[0, pl.ds(row, BLK), oc:oc + LANES] = lse

    for m in range(r):
        for pair in range(HEADS_PER_GROUP // 2):
            tile(m, pair, 0, None)
        if nb > 1:
            def body(n, carry):
                row = pl.multiple_of(n * BLK, BLK)
                prev = pl.multiple_of(row - BLK, BLK)
                for pair in range(HEADS_PER_GROUP // 2):
                    tile(m, pair, row, prev)
                return carry
            lax.fori_loop(1, nb, body, 0)


def _dil_call(p, bias, batch, seq, r):
    length = seq // r
    nb = length // BLK
    pv = p.reshape(batch, length, r * 3 * GROUP_W)
    return pl.pallas_call(
        functools.partial(_dil_kernel, r=r, nb=nb),
        grid=(batch,),
        in_specs=[pl.BlockSpec((1, length, r * 3 * GROUP_W), lambda b: (b, 0, 0)),
                  _resident(bias.shape, lambda b: (0, 0, 0))],
        out_specs=[pl.BlockSpec((1, length, r * GROUP_W), lambda b: (b, 0, 0))] * 2,
        out_shape=[jax.ShapeDtypeStruct((batch, length, r * GROUP_W), BF16),
                   jax.ShapeDtypeStruct((batch, length, r * GROUP_W), F32)],
        compiler_params=_params(("arbitrary",)),
        name=f"dilated_r{r}",
    )(pv, bias)


def _diff_kernel(qtab_ref, ktab_ref, lam_ref, q_ref, k_ref, vt_ref, bias_ref, sg_ref, o_ref, m_s, l_s, acc_s, *,
                 lam_init):
    step = pl.program_id(2)
    qi = qtab_ref[step]
    ki = ktab_ref[step]

    @pl.when(ki == 0)
    def _():
        m_s[...] = jnp.full(m_s.shape, NEG_INF, F32)
        l_s[...] = jnp.zeros(l_s.shape, F32)
        acc_s[...] = jnp.zeros(acc_s.shape, F32)

    q = q_ref[0]
    k = k_ref[0]
    vt = vt_ref[...]
    bias = bias_ref[0, qi - ki]
    lane = lax.broadcasted_iota(jnp.int32, q.shape, 1)
    for t in range(2):
        qm = jnp.where((lane < HEAD_DIM) if t == 0 else (lane >= HEAD_DIM), q, jnp.zeros_like(q))
        s = lax.dot_general(k, qm, NT_DIMS, preferred_element_type=F32) + bias
        m_prev = m_s[t]
        m_new = jnp.maximum(m_prev, jnp.max(s, axis=0, keepdims=True))
        alpha = jnp.exp(m_prev - m_new)
        p = jnp.exp(s - m_new)
        l_s[t] = alpha * l_s[t] + jnp.sum(p, axis=0, keepdims=True)
        acc_s[t] = alpha * acc_s[t] + jnp.dot(vt, p.astype(BF16), preferred_element_type=F32)
        m_s[t] = m_new

    @pl.when(ki == qi)
    def _():
        lv = lam_ref[...]
        lam = (jnp.exp(jnp.sum(lv[0:1] * lv[1:2], axis=-1, keepdims=True))
               - jnp.exp(jnp.sum(lv[2:3] * lv[3:4], axis=-1, keepdims=True)) + lam_init)
        o = acc_s[0] / l_s[0] - lam * (acc_s[1] / l_s[1])
        ms = jnp.mean(o * o, axis=0, keepdims=True)
        o = (o * lax.rsqrt(ms + SUBLN_EPS) * sg_ref[...]) * (1.0 - lam_init)
        o_ref[0] = o.T.astype(BF16)


def _diff_call(qk, vt, bias, lam_vecs, subln_g, batch, seq, tq, lam_init):
    nq = seq // tq
    hw = 2 * HEAD_DIM
    qk3 = qk.reshape(batch, seq, 2 * DIFF_W)
    pairs = [(i, j) for i in range(nq) for j in range(i + 1)]
    qtab = jnp.asarray([p[0] for p in pairs], jnp.int32)
    ktab = jnp.asarray([p[1] for p in pairs], jnp.int32)
    grid_spec = pltpu.PrefetchScalarGridSpec(
        num_scalar_prefetch=2,
        grid=(DIFF_HEADS, batch, len(pairs)),
        in_specs=[pl.BlockSpec((4, HEAD_DIM), lambda h, b, s, qt, kt: (0, 0)),
                  pl.BlockSpec((1, tq, hw), lambda h, b, s, qt, kt: (b, qt[s], h)),
                  pl.BlockSpec((1, tq, hw), lambda h, b, s, qt, kt: (b, kt[s], DIFF_HEADS + h)),
                  pl.BlockSpec((hw, tq), lambda h, b, s, qt, kt: (h, b * nq + kt[s])),
                  pl.BlockSpec((1, nq, tq, tq), lambda h, b, s, qt, kt: (h, 0, 0, 0)),
                  pl.BlockSpec((hw, 1), lambda h, b, s, qt, kt: (0, 0))],
        out_specs=pl.BlockSpec((1, tq, hw), lambda h, b, s, qt, kt: (b, qt[s], h)),
        scratch_shapes=[pltpu.VMEM((2, 1, tq), F32), pltpu.VMEM((2, 1, tq), F32), pltpu.VMEM((2, hw, tq), F32)])
    return pl.pallas_call(
        functools.partial(_diff_kernel, lam_init=lam_init),
        grid_spec=grid_spec,
        out_shape=jax.ShapeDtypeStruct((batch, seq, DIFF_W), BF16),
        compiler_params=_params(("arbitrary", "arbitrary", "arbitrary")),
        name="diff_attn",
    )(qtab, ktab, lam_vecs, qk3, qk3, vt, bias, subln_g).reshape(batch * seq, DIFF_W)


def _merge_kernel(o1, o2, o3, l1, l2, l3, ob_ref, gate_ref, x_ref, g1_ref, sh2_ref, sc2_ref, n2_ref,
                  wpa_ref, wpb_ref, wo_ref, wq_ref, sk_ref, x1_ref, h2_ref, sct_ref, tok_s):
    tm = x_ref.shape[0]

    def token_major(ref, g, slot):
        r = DIL_GROUPS[g][1]
        if r == 1:
            return ref[0].astype(F32)
        nk = GROUP_W // LANES
        for m in range(r):
            for k in range(nk):
                c0 = m * GROUP_W + k * LANES
                tok_s[slot, k, pl.ds(m, tm // r, stride=r), :] = ref[0, :, c0:c0 + LANES].astype(F32)
        return jnp.concatenate([tok_s[slot, k] for k in range(nk)], axis=1)

    la, lb, lc = token_major(l1, 0, 0), token_major(l2, 1, 0), token_major(l3, 2, 1)
    oa_, ob_, oc_ = token_major(o1, 0, 0), token_major(o2, 1, 2), token_major(o3, 2, 3)
    mx = jnp.maximum(jnp.maximum(la, lb), lc)
    ea, eb, ec = jnp.exp(la - mx), jnp.exp(lb - mx), jnp.exp(lc - mx)
    oa = (ea * oa_ + eb * ob_ + ec * oc_) / (ea + eb + ec)
    pa = jnp.dot(oa.astype(BF16), wpa_ref[...], preferred_element_type=F32)
    pb = jnp.dot(ob_ref[...], wpb_ref[...], preferred_element_type=F32)
    ga = jax.nn.sigmoid(gate_ref[:, 0:D_MODEL].astype(F32))
    gb = jax.nn.sigmoid(gate_ref[:, D_MODEL:2 * D_MODEL].astype(F32))
    merged = ga * pa + gb * pb
    y = jnp.dot(merged.astype(BF16), wo_ref[...], preferred_element_type=F32)
    x1 = x_ref[...] + g1_ref[0] * y
    x1_ref[...] = x1
    h2 = _modulated_norm(x1, n2_ref[...], sc2_ref[0], sh2_ref[0]).astype(BF16)
    h2_ref[...] = h2
    q = jnp.dot(h2, wq_ref[...], preferred_element_type=F32).astype(BF16)
    for hp in range(2 * PEER_HEADS):
        sct_ref[hp] = lax.dot_general(sk_ref[hp % 2], q[:, hp * PEER_DK:(hp + 1) * PEER_DK], NT_DIMS,
                                      preferred_element_type=F32)


def _merge_call(os_, ls_, ob, gates, x2, mod_l, n2, wpa, wpb, wo, wq, sk, seq, tm):
    t, d = x2.shape
    nper = seq // tm
    mod_spec = lambda k: pl.BlockSpec((1, 1, d), lambda i: (i // nper, 0, k))
    row = lambda w: pl.BlockSpec((tm, w), lambda i: (i, 0))
    full = lambda a: _resident(a.shape, lambda i: (0,) * a.ndim)
    dil_spec = lambda g: pl.BlockSpec((1, tm // DIL_GROUPS[g][1], DIL_GROUPS[g][1] * GROUP_W),
                                      lambda i: (i // nper, i % nper, 0))
    return pl.pallas_call(
        _merge_kernel,
        grid=(t // tm,),
        in_specs=[dil_spec(0), dil_spec(1), dil_spec(2)] * 2
        + [row(DIFF_W), row(2 * D_MODEL), row(d), mod_spec(2), mod_spec(3), mod_spec(4),
                                       pl.BlockSpec((1, d), lambda i: (0, 0)),
                                       full(wpa), full(wpb), full(wo), full(wq), full(sk)],
        out_specs=[row(d), row(d), pl.BlockSpec((2 * PEER_HEADS, N_KEYS, tm), lambda i: (0, 0, i))],
        out_shape=[jax.ShapeDtypeStruct((t, d), F32), jax.ShapeDtypeStruct((t, d), BF16),
                   jax.ShapeDtypeStruct((2 * PEER_HEADS, N_KEYS, t), F32)],
        scratch_shapes=[pltpu.VMEM((4, GROUP_W // LANES, tm, LANES), F32)],
        compiler_params=_params(("arbitrary",)),
        name="merge_proj",
    )(*os_, *ls_, ob, gates, x2, mod_l, mod_l, mod_l, n2, wpa, wpb, wo, wq, sk)


def _colmax(a):
    return jnp.max(a, axis=0, keepdims=True)


def _bf16_pair_words(x):
    hi = pltpu.bitcast(x.astype(BF16).astype(F32), jnp.uint32)
    return hi | (hi >> 16)


def _bf16_half_words(x):
    n = x.shape[0] // 2
    bits = pltpu.bitcast(x.astype(BF16).astype(F32), jnp.uint32)
    return (bits[0:n] >> 16) | (bits[n:2 * n] & jnp.uint32(0xFFFF0000))


def _second_key_order(table):
    e, d = table.shape
    half = N_KEYS // 2
    return table.reshape(e // N_KEYS, 2, half, d).transpose(0, 2, 1, 3).reshape(e, d)


def _packed_rows(word_row, rows):
    v = pltpu.bitcast(jnp.broadcast_to(word_row, (8, word_row.shape[-1])), BF16)
    return jnp.tile(v, (rows // v.shape[0], 1))


def _topk_kernel(sct_ref, n_ref, c_ref, r_ref, w_ref):
    tl = sct_ref.shape[-1]
    k = PEER_TOPK
    row16 = lax.broadcasted_iota(jnp.int32, (k, tl), 0)
    for h in range(PEER_HEADS):
        s0 = sct_ref[2 * h]
        s1 = sct_ref[2 * h + 1]

        def top_sorted(s, want_rank):
            work = s
            vals = jnp.full((k, tl), NEG_INF, F32)
            rank = jnp.full(s.shape, float(k), F32)
            for rr in range(k):
                v = _colmax(work)
                eq = work == v
                if want_rank:
                    rank = jnp.where(eq, float(rr), rank)
                work = jnp.where(eq, NEG_INF, work)
                vals = jnp.where(row16 == rr, v, vals)
            return vals, rank

        a, _ = top_sorted(s0, False)
        b, r1 = top_sorted(s1, True)
        cands = []
        for x in range(k):
            ny = k // (x + 1)
            cands.append(jnp.where(row16 < ny, a[x:x + 1] + b, NEG_INF))
        work = list(cands)
        c16 = None
        for rr in range(k):
            v = work[0]
            for w_ in work[1:]:
                v = jnp.maximum(v, w_)
            v = _colmax(v)
            work = [jnp.where(w_ == v, NEG_INF, w_) for w_ in work]
            c16 = v
        smax = a[0:1] + b[0:1]
        z = jnp.zeros((1, tl), F32)
        for cd in cands:
            z = z + jnp.sum(jnp.where(cd >= c16, jnp.exp(cd - smax), 0.0), axis=0, keepdims=True)
        n = jnp.zeros(s0.shape, F32)
        for y in range(k):
            n = n + jnp.where(s0 + b[y:y + 1] >= c16, 1.0, 0.0)
        n_ref[h] = _bf16_pair_words(n)
        c_ref[h] = _bf16_pair_words(jnp.exp(s0 - a[0:1]) / z)
        r_ref[h] = _bf16_half_words(r1)
        w_ref[h] = _bf16_half_words(jnp.exp(s1 - b[0:1]))


def _topk_call(sct, tl):
    hp, nk, t = sct.shape
    spec = lambda rows: pl.BlockSpec((PEER_HEADS, rows, tl), lambda i: (0, 0, i))
    return pl.pallas_call(
        _topk_kernel,
        grid=(t // tl,),
        in_specs=[pl.BlockSpec((hp, nk, tl), lambda i: (0, 0, i))],
        out_specs=[spec(nk), spec(nk), spec(nk // 2), spec(nk // 2)],
        out_shape=[jax.ShapeDtypeStruct((PEER_HEADS, nk, t), jnp.uint32)] * 2
        + [jax.ShapeDtypeStruct((PEER_HEADS, nk // 2, t), jnp.uint32)] * 2,
        compiler_params=_params(("arbitrary",)),
        name="peer_topk",
    )(sct)


def _gelu_tanh(x):
    return 0.5 * x * (1.0 + jnp.tanh(math.sqrt(2.0 / math.pi) * (x + 0.044715 * (x * x * x))))


PEER_COLS = 512


def _peer_kernel(h2_ref, u_ref, vt_ref, n_ref, c_ref, r_ref, w_ref, x1_ref, g2_ref, o_ref, acc_s, ga_s, *, n_first):
    e = pl.program_id(1)
    tm = h2_ref.shape[0]

    @pl.when(e == 0)
    def _():
        acc_s[...] = jnp.zeros(acc_s.shape, F32)

    for cb in range(tm // PEER_COLS):
        blk = slice(cb * PEER_COLS, (cb + 1) * PEER_COLS)
        pre = lax.dot_general(u_ref[...], h2_ref[blk, :], NT_DIMS, preferred_element_type=F32)
        act = _gelu_tanh(pre.astype(BF16))
        for ch in range(PEER_COLS // LANES):
            cols = slice(cb * PEER_COLS + ch * LANES, cb * PEER_COLS + (ch + 1) * LANES)
            for ip in range(n_first // 2):
                gts = [jnp.zeros((N_KEYS, LANES), BF16), jnp.zeros((N_KEYS, LANES), BF16)]
                for h in range(PEER_HEADS):
                    r_t = pltpu.bitcast(r_ref[h, :, cols], BF16)
                    w_t = pltpu.bitcast(w_ref[h, :, cols], BF16)
                    for k in range(2):
                        ii = 2 * ip + k
                        n_row = _packed_rows(n_ref[h, ii:ii + 1, cols], N_KEYS)
                        c_row = _packed_rows(c_ref[h, ii:ii + 1, cols], N_KEYS)
                        gts[k] = gts[k] + jnp.where(r_t < n_row, w_t * c_row, jnp.zeros_like(w_t))
                for k in range(2):
                    rows = slice((2 * ip + k) * N_KEYS, (2 * ip + k + 1) * N_KEYS)
                    ga_s[rows, cols] = gts[k] * act[rows, ch * LANES:(ch + 1) * LANES]
        acc_s[:, blk] += jnp.dot(vt_ref[...], ga_s[:, blk], preferred_element_type=F32)

    @pl.when(e == pl.num_programs(1) - 1)
    def _():
        o_ref[...] = x1_ref[...] + g2_ref[0] * acc_s[...].T


def _peer_call(h2, u_bf, vt_bf, n, c, r, w, x1, mod_l, seq, tm, te):
    t, d = x1.shape
    nper = seq // tm
    n_first = te // N_KEYS
    sel = pl.BlockSpec((PEER_HEADS, N_KEYS // 2, tm), lambda i, e: (0, 0, i))
    first = pl.BlockSpec((PEER_HEADS, n_first, tm), lambda i, e: (0, e, i))
    return pl.pallas_call(
        functools.partial(_peer_kernel, n_first=n_first),
        grid=(t // tm, u_bf.shape[0] // te),
        in_specs=[pl.BlockSpec((tm, d), lambda i, e: (i, 0)),
                  pl.BlockSpec((te, d), lambda i, e: (e, 0)),
                  pl.BlockSpec((d, te), lambda i, e: (0, e)),
                  first, first, sel, sel,
                  pl.BlockSpec((tm, d), lambda i, e: (i, 0)),
                  pl.BlockSpec((1, 1, d), lambda i, e: (i // nper, 0, 5))],
        out_specs=pl.BlockSpec((tm, d), lambda i, e: (i, 0)),
        out_shape=jax.ShapeDtypeStruct((t, d), F32),
        scratch_shapes=[pltpu.VMEM((d, tm), F32), pltpu.VMEM((te, tm), BF16)],
        compiler_params=_params(("arbitrary", "arbitrary")),
        name="peer_dense",
    )(h2, u_bf, vt_bf, n, c, r, w, x1, mod_l)


def _final_kernel(x_ref, g_ref, o_ref):
    x = x_ref[...]
    ms = jnp.mean(x * x, axis=-1, keepdims=True)
    o_ref[...] = x * lax.rsqrt(ms + NORM_EPS) * g_ref[...]


def _final_call(x2, g, tm):
    t, d = x2.shape
    return pl.pallas_call(
        _final_kernel,
        grid=(t // tm,),
        in_specs=[pl.BlockSpec((tm, d), lambda i: (i, 0)), pl.BlockSpec((1, d), lambda i: (0, 0))],
        out_specs=pl.BlockSpec((tm, d), lambda i: (i, 0)),
        out_shape=jax.ShapeDtypeStruct((t, d), F32),
        compiler_params=_params(("arbitrary",)),
        name="final_norm",
    )(x2, g)


TOKEN_TILE = 512
DIFF_TILE = 512
TOPK_LANES = 128
EXPERT_TILE = 1024


def _mixer_layer(x2, mod_l, layer, lam_init, batch, seq, w_in_bf, w_vt_bf, norm1_g, dil_bias, diff_bias, lam_vecs,
                 subln_g, norm2_g, wpa, wpb, wo, wq, sk):
    p1, p2, p3, qk, vt, gates = _in_call(x2, mod_l, norm1_g, w_in_bf, w_vt_bf, seq, TOKEN_TILE)
    os_, ls_ = [], []
    for g, p in enumerate((p1, p2, p3)):
        o, lse = _dil_call(p, dil_bias[g], batch, seq, DIL_GROUPS[g][1])
        os_.append(o)
        ls_.append(lse)
    ob = _diff_call(qk, vt, diff_bias, lam_vecs, subln_g, batch, seq, DIFF_TILE, lam_init)
    return _merge_call(os_, ls_, ob, gates, x2, mod_l, norm2_g, wpa, wpb, wo, wq, sk, seq, TOKEN_TILE)


def _peer_layer(x1, h2, sct, mod_l, u_bf, vt_bf, seq):
    n, c, r, w = _topk_call(sct, TOPK_LANES)
    return _peer_call(h2, u_bf, vt_bf, n, c, r, w, x1, mod_l, seq, TOKEN_TILE, EXPERT_TILE)


def kernel(x, c, w_ada, b_ada, norm1_g, norm2_g, w_in, w_proj_a, w_proj_b, w_out, lam_q1, lam_k1, lam_q2, lam_k2,
           subln_g, rel_bias, peer_wq, peer_subkeys, peer_u, peer_v, final_g):
    batch, seq, d = x.shape
    depth = w_ada.shape[0]
    x2 = x.reshape(batch * seq, d)
    mods = _ada_call(c, w_ada, b_ada)
    dil_bias = [_dil_bias_table(rel_bias, g, DIL_GROUPS[g][1]) for g in range(N_DIL_GROUPS)]
    diff_bias = _diff_bias_table(rel_bias, DIFF_TILE, seq // DIFF_TILE)
    for l in range(depth):
        lam_init = 0.8 - 0.6 * math.exp(-0.3 * l)
        mod_l = mods[l].reshape(batch, 1, 6 * d)
        lam_vecs = jnp.stack([lam_q1[l], lam_k1[l], lam_q2[l], lam_k2[l]])
        v_cols = slice(3 * DIL_WIDTH + 2 * DIFF_W, 3 * DIL_WIDTH + 3 * DIFF_W)
        x1, h2, sct = _mixer_layer(
            x2, mod_l, l, lam_init, batch, seq, w_in[l].astype(BF16), w_in[l][:, v_cols].T.astype(BF16),
            norm1_g[l].reshape(1, d), dil_bias, diff_bias,
            lam_vecs, subln_g[l].reshape(2 * HEAD_DIM, 1), norm2_g[l].reshape(1, d), w_proj_a[l].astype(BF16),
            w_proj_b[l].astype(BF16), w_out[l].astype(BF16), peer_wq[l].astype(BF16), peer_subkeys[l].astype(BF16))
        x2 = _peer_layer(x1, h2, sct, mod_l, _second_key_order(peer_u[l]).astype(BF16),
                         _second_key_order(peer_v[l]).T.astype(BF16), seq)
    return _final_call(x2, final_g.reshape(1, d), TOKEN_TILE).reshape(batch, seq, d)
```

```python
import functools
import math

import numpy as np
import jax
import jax.numpy as jnp
from jax import lax
from jax.experimental import pallas as pl
from jax.experimental.pallas import tpu as pltpu

F32 = jnp.float32
BF16 = jnp.bfloat16

D_MODEL = 1024
HEAD_DIM = 64
DIL_GROUPS = ((128, 1), (512, 4), (2048, 16))
N_DIL_GROUPS = 3
HEADS_PER_GROUP = 8
GROUP_W = HEADS_PER_GROUP * HEAD_DIM
DIL_WIDTH = N_DIL_GROUPS * GROUP_W
DIFF_HEADS = 8
DIFF_W = DIFF_HEADS * 2 * HEAD_DIM
BLK = 128
NUM_BUCKETS = 32
MAX_DISTANCE = 2048
IN_WIDTH = 3 * DIL_WIDTH + 3 * DIFF_W + 2 * D_MODEL
PEER_HEADS = 8
PEER_DK = 128
N_KEYS = 128
PEER_TOPK = 16
NORM_EPS = 1e-6
SUBLN_EPS = 1e-5
NEG_INF = -1e30
Q_SCALE = HEAD_DIM ** -0.5

LANES = 128
VMEM_LIMIT = 56 * 1024 * 1024

NT_DIMS = (((1,), (1,)), ((), ()))


def _params(sem, vmem=VMEM_LIMIT):
    return pltpu.CompilerParams(dimension_semantics=sem, vmem_limit_bytes=vmem)


def _resident(shape, index_map):
    return pl.BlockSpec(shape, index_map, pipeline_mode=pl.Buffered(1))


def _ada_kernel(c_ref, w_ref, b_ref, o_ref):
    c = c_ref[...]
    cond = c * jax.nn.sigmoid(c)
    o_ref[0] = jnp.dot(cond, w_ref[0], preferred_element_type=F32) + b_ref[0]


def _ada_call(c, w_ada, b_ada):
    depth, d, n6 = w_ada.shape
    b = c.shape[0]
    nchunk = n6 // d
    return pl.pallas_call(
        _ada_kernel,
        grid=(depth, nchunk),
        in_specs=[pl.BlockSpec((b, d), lambda l, j: (0, 0)),
                  pl.BlockSpec((1, d, d), lambda l, j: (l, 0, j)),
                  pl.BlockSpec((1, 1, d), lambda l, j: (l, 0, j))],
        out_specs=pl.BlockSpec((1, b, d), lambda l, j: (l, 0, j)),
        out_shape=jax.ShapeDtypeStruct((depth, b, n6), F32),
        compiler_params=_params(("arbitrary", "arbitrary")),
        name="ada_mod",
    )(c, w_ada, b_ada.reshape(depth, 1, n6))


def _modulated_norm(x, g, sc, sh):
    ms = jnp.mean(x * x, axis=-1, keepdims=True)
    return (x * lax.rsqrt(ms + NORM_EPS) * g) * (1.0 + sc) + sh


def _in_kernel(x_ref, g_ref, sc_ref, sh_ref, w_ref, wvt_ref, p1_ref, p2_ref, p3_ref, qk_ref, vt_ref, gate_ref,
               stage_s):
    hb = _modulated_norm(x_ref[...], g_ref[...], sc_ref[0], sh_ref[0]).astype(BF16)
    tm = hb.shape[0]

    def mm32(c0, scale=None):
        r = jnp.dot(hb, w_ref[:, c0:c0 + GROUP_W], preferred_element_type=F32)
        return r if scale is None else r * scale

    def mm(c0, scale=None):
        return mm32(c0, scale).astype(BF16)

    p1_ref[:, 0:GROUP_W] = mm(0, Q_SCALE)
    p1_ref[:, GROUP_W:2 * GROUP_W] = mm(DIL_WIDTH)
    p1_ref[:, 2 * GROUP_W:3 * GROUP_W] = mm(2 * DIL_WIDTH)
    for g, p_ref in ((1, p2_ref), (2, p3_ref)):
        r = DIL_GROUPS[g][1]
        for piece in range(3):
            res = mm32(piece * DIL_WIDTH + g * GROUP_W, Q_SCALE if piece == 0 else None)
            for k in range(GROUP_W // LANES):
                stage_s[k] = res[:, k * LANES:(k + 1) * LANES]
            for m in range(r):
                for k in range(GROUP_W // LANES):
                    c0 = m * 3 * GROUP_W + piece * GROUP_W + k * LANES
                    p_ref[0, :, c0:c0 + LANES] = stage_s[k, pl.ds(m, tm // r, stride=r), :].astype(BF16)
    base = 3 * DIL_WIDTH
    for j in range(2 * DIFF_W // GROUP_W):
        qk_ref[:, j * GROUP_W:(j + 1) * GROUP_W] = mm(base + j * GROUP_W, Q_SCALE if j < DIFF_W // GROUP_W else None)
    vt_ref[...] = lax.dot_general(wvt_ref[...], hb, NT_DIMS, preferred_element_type=F32).astype(BF16)
    base += 3 * DIFF_W
    for j in range(2 * D_MODEL // GROUP_W):
        gate_ref[:, j * GROUP_W:(j + 1) * GROUP_W] = mm(base + j * GROUP_W)


def _in_call(x2, mod_l, g1, w_in_bf, w_vt_bf, seq, tm):
    t, d = x2.shape
    nper = seq // tm
    mod_spec = lambda k: pl.BlockSpec((1, 1, d), lambda i: (i // nper, 0, k))
    row = lambda w: pl.BlockSpec((tm, w), lambda i: (i, 0))
    batch = t // seq
    pw = 3 * GROUP_W
    dil_spec = lambda r, w: pl.BlockSpec((1, tm // r, r * w), lambda i: (i // nper, i % nper, 0))
    dil_shape = lambda r, w, dt: jax.ShapeDtypeStruct((batch, seq // r, r * w), dt)
    r2, r3 = DIL_GROUPS[1][1], DIL_GROUPS[2][1]
    return pl.pallas_call(
        _in_kernel,
        grid=(t // tm,),
        in_specs=[row(d), pl.BlockSpec((1, d), lambda i: (0, 0)), mod_spec(1), mod_spec(0),
                  _resident((d, IN_WIDTH), lambda i: (0, 0)), _resident((DIFF_W, d), lambda i: (0, 0))],
        out_specs=[row(pw), dil_spec(r2, pw), dil_spec(r3, pw), row(2 * DIFF_W),
                   pl.BlockSpec((DIFF_W, tm), lambda i: (0, i)), row(2 * D_MODEL)],
        out_shape=[jax.ShapeDtypeStruct((t, pw), BF16), dil_shape(r2, pw, BF16), dil_shape(r3, pw, BF16),
                   jax.ShapeDtypeStruct((t, 2 * DIFF_W), BF16), jax.ShapeDtypeStruct((DIFF_W, t), BF16),
                   jax.ShapeDtypeStruct((t, 2 * D_MODEL), BF16)],
        scratch_shapes=[pltpu.VMEM((GROUP_W // LANES, tm, LANES), F32)],
        compiler_params=_params(("arbitrary",)),
        name="in_proj",
    )(x2, g1, mod_l, mod_l, w_in_bf, w_vt_bf)


def _t5_bucket(dist):
    max_exact = NUM_BUCKETS // 2
    d = jnp.maximum(dist, 0)
    ratio = jnp.log(jnp.maximum(d, 1).astype(F32) / max_exact) / math.log(MAX_DISTANCE / max_exact)
    large = jnp.minimum(max_exact + (ratio * (NUM_BUCKETS - max_exact)).astype(jnp.int32), NUM_BUCKETS - 1)
    return jnp.where(d < max_exact, d, large)


def _toeplitz(u, n):
    heads, m = u.shape
    return jnp.tile(u, (1, n))[:, :n * (m - 1)].reshape(heads, n, m - 1)


def _dil_bias_table(rel_bias, g, r):
    n_back = DIL_GROUPS[g][0] // r
    assert n_back <= BLK
    delta = BLK - jnp.arange(3 * BLK + 1)
    tab = rel_bias[:, g * HEADS_PER_GROUP:(g + 1) * HEADS_PER_GROUP]
    u = jnp.where(((delta >= 0) & (delta <= n_back))[None], tab[_t5_bucket(delta * r)].astype(F32).T, NEG_INF)
    return _toeplitz(u, BLK)[:, :, :2 * BLK]


def _diff_bias_table(rel_bias, tq, nd):
    dist = jnp.arange((nd + 1) * tq + 1) - (tq - 1)
    tab = rel_bias[:, N_DIL_GROUPS * HEADS_PER_GROUP:]
    u = jnp.where((dist >= 0)[None], tab[_t5_bucket(dist)].astype(F32).T, NEG_INF)
    a = _toeplitz(u, tq)
    return jnp.stack([a[:, :, d * tq + tq - 1:d * tq + 2 * tq - 1] for d in range(nd)], axis=1)


def _dil_pair(q2, kc, vc, kp, vp, bias_ref, pair):
    lane = lax.broadcasted_iota(jnp.int32, (BLK, LANES), 1)
    low = lane < HEAD_DIM
    outs, lses = [], []
    for sub in range(2):
        head = 2 * pair + sub
        qm = jnp.where(low if sub == 0 else jnp.logical_not(low), q2, jnp.zeros_like(q2))
        sc = lax.dot_general(qm, kc, NT_DIMS, preferred_element_type=F32) + bias_ref[head, :, BLK:2 * BLK]
        mx = jnp.max(sc, axis=-1, keepdims=True)
        if kp is not None:
            sp = lax.dot_general(qm, kp, NT_DIMS, preferred_element_type=F32) + bias_ref[head, :, 0:BLK]
            mx = jnp.maximum(mx, jnp.max(sp, axis=-1, keepdims=True))
        pc = jnp.exp(sc - mx)
        den = jnp.sum(pc, axis=-1, keepdims=True)
        acc = jnp.dot(pc.astype(BF16), vc, preferred_element_type=F32)
        if kp is not None:
            pp = jnp.exp(sp - mx)
            den = den + jnp.sum(pp, axis=-1, keepdims=True)
            acc = acc + jnp.dot(pp.astype(BF16), vp, preferred_element_type=F32)
        outs.append(acc / den)
        lses.append(mx + jnp.log(den))
    o = jnp.where(low, outs[0], outs[1])
    lse = jnp.where(low, lses[0], lses[1])
    return o, lse


def _dil_kernel(p_ref, bias_ref, o_ref, lse_ref, *, r, nb):
    pw = 3 * GROUP_W

    def tile(m, pair, row, prev_row):
        qc = m * pw + pair * LANES
        kc_ = qc + GROUP_W
        vc_ = qc + 2 * GROUP_W
        oc = m * GROUP_W + pair * LANES
        q2 = p_ref[0, pl.ds(row, BLK), qc:qc + LANES]
        kc = p_ref[0, pl.ds(row, BLK), kc_:kc_ + LANES]
        vc = p_ref[0, pl.ds(row, BLK), vc_:vc_ + LANES]
        if prev_row is None:
            kp = vp = None
        else:
            kp = p_ref[0, pl.ds(prev_row, BLK), kc_:kc_ + LANES]
            vp = p_ref[0, pl.ds(prev_row, BLK), vc_:vc_ + LANES]
        o, lse = _dil_pair(q2, kc, vc, kp, vp, bias_ref, pair)
        o_ref[0, pl.ds(row, BLK), oc:oc + LANES] = o.astype(BF16)
        lse_ref[0, pl.ds(row, BLK), oc:oc + LANES] = lse

    for m in range(r):
        for pair in range(HEADS_PER_GROUP // 2):
            tile(m, pair, 0, None)
        if nb > 1:
            def body(n, carry):
                row = pl.multiple_of(n * BLK, BLK)
                prev = pl.multiple_of(row - BLK, BLK)
                for pair in range(HEADS_PER_GROUP // 2):
                    tile(m, pair, row, prev)
                return carry
            lax.fori_loop(1, nb, body, 0)


def _dil_call(p, bias, batch, seq, r):
    length = seq // r
    nb = length // BLK
    pv = p.reshape(batch, length, r * 3 * GROUP_W)
    return pl.pallas_call(
        functools.partial(_dil_kernel, r=r, nb=nb),
        grid=(batch,),
        in_specs=[pl.BlockSpec((1, length, r * 3 * GROUP_W), lambda b: (b, 0, 0)),
                  _resident(bias.shape, lambda b: (0, 0, 0))],
        out_specs=[pl.BlockSpec((1, length, r * GROUP_W), lambda b: (b, 0, 0))] * 2,
        out_shape=[jax.ShapeDtypeStruct((batch, length, r * GROUP_W), BF16),
                   jax.ShapeDtypeStruct((batch, length, r * GROUP_W), F32)],
        compiler_params=_params(("arbitrary",)),
        name=f"dilated_r{r}",
    )(pv, bias)


def _diff_kernel(qtab_ref, ktab_ref, lam_ref, q_ref, k_ref, vt_ref, bias_ref, sg_ref, o_ref, m_s, l_s, acc_s, *,
                 lam_init):
    step = pl.program_id(2)
    qi = qtab_ref[step]
    ki = ktab_ref[step]

    @pl.when(ki == 0)
    def _():
        m_s[...] = jnp.full(m_s.shape, NEG_INF, F32)
        l_s[...] = jnp.zeros(l_s.shape, F32)
        acc_s[...] = jnp.zeros(acc_s.shape, F32)

    q = q_ref[0]
    k = k_ref[0]
    vt = vt_ref[...]
    bias = bias_ref[0, qi - ki]
    lane = lax.broadcasted_iota(jnp.int32, q.shape, 1)
    for t in range(2):
        qm = jnp.where((lane < HEAD_DIM) if t == 0 else (lane >= HEAD_DIM), q, jnp.zeros_like(q))
        s = lax.dot_general(k, qm, NT_DIMS, preferred_element_type=F32) + bias
        m_prev = m_s[t]
        m_new = jnp.maximum(m_prev, jnp.max(s, axis=0, keepdims=True))
        alpha = jnp.exp(m_prev - m_new)
        p = jnp.exp(s - m_new)
        l_s[t] = alpha * l_s[t] + jnp.sum(p, axis=0, keepdims=True)
        acc_s[t] = alpha * acc_s[t] + jnp.dot(vt, p.astype(BF16), preferred_element_type=F32)
        m_s[t] = m_new

    @pl.when(ki == qi)
    def _():
        lv = lam_ref[...]
        lam = (jnp.exp(jnp.sum(lv[0:1] * lv[1:2], axis=-1, keepdims=True))
               - jnp.exp(jnp.sum(lv[2:3] * lv[3:4], axis=-1, keepdims=True)) + lam_init)
        o = acc_s[0] / l_s[0] - lam * (acc_s[1] / l_s[1])
        ms = jnp.mean(o * o, axis=0, keepdims=True)
        o = (o * lax.rsqrt(ms + SUBLN_EPS) * sg_ref[...]) * (1.0 - lam_init)
        o_ref[0] = o.T.astype(BF16)


def _diff_call(qk, vt, bias, lam_vecs, subln_g, batch, seq, tq, lam_init):
    nq = seq // tq
    hw = 2 * HEAD_DIM
    qk3 = qk.reshape(batch, seq, 2 * DIFF_W)
    pairs = [(i, j) for i in range(nq) for j in range(i + 1)]
    qtab = jnp.asarray([p[0] for p in pairs], jnp.int32)
    ktab = jnp.asarray([p[1] for p in pairs], jnp.int32)
    grid_spec = pltpu.PrefetchScalarGridSpec(
        num_scalar_prefetch=2,
        grid=(DIFF_HEADS, batch, len(pairs)),
        in_specs=[pl.BlockSpec((4, HEAD_DIM), lambda h, b, s, qt, kt: (0, 0)),
                  pl.BlockSpec((1, tq, hw), lambda h, b, s, qt, kt: (b, qt[s], h)),
                  pl.BlockSpec((1, tq, hw), lambda h, b, s, qt, kt: (b, kt[s], DIFF_HEADS + h)),
                  pl.BlockSpec((hw, tq), lambda h, b, s, qt, kt: (h, b * nq + kt[s])),
                  pl.BlockSpec((1, nq, tq, tq), lambda h, b, s, qt, kt: (h, 0, 0, 0)),
                  pl.BlockSpec((hw, 1), lambda h, b, s, qt, kt: (0, 0))],
        out_specs=pl.BlockSpec((1, tq, hw), lambda h, b, s, qt, kt: (b, qt[s], h)),
        scratch_shapes=[pltpu.VMEM((2, 1, tq), F32), pltpu.VMEM((2, 1, tq), F32), pltpu.VMEM((2, hw, tq), F32)])
    return pl.pallas_call(
        functools.partial(_diff_kernel, lam_init=lam_init),
        grid_spec=grid_spec,
        out_shape=jax.ShapeDtypeStruct((batch, seq, DIFF_W), BF16),
        compiler_params=_params(("arbitrary", "arbitrary", "arbitrary")),
        name="diff_attn",
    )(qtab, ktab, lam_vecs, qk3, qk3, vt, bias, subln_g).reshape(batch * seq, DIFF_W)


def _merge_kernel(o1, o2, o3, l1, l2, l3, ob_ref, gate_ref, x_ref, g1_ref, sh2_ref, sc2_ref, n2_ref,
                  wpa_ref, wpb_ref, wo_ref, wq_ref, sk_ref, x1_ref, h2_ref, sct_ref, tok_s):
    tm = x_ref.shape[0]

    def token_major(ref, g, slot):
        r = DIL_GROUPS[g][1]
        if r == 1:
            return ref[0].astype(F32)
        nk = GROUP_W // LANES
        for m in range(r):
            for k in range(nk):
                c0 = m * GROUP_W + k * LANES
                tok_s[slot, k, pl.ds(m, tm // r, stride=r), :] = ref[0, :, c0:c0 + LANES].astype(F32)
        return jnp.concatenate([tok_s[slot, k] for k in range(nk)], axis=1)

    la, lb, lc = token_major(l1, 0, 0), token_major(l2, 1, 0), token_major(l3, 2, 1)
    oa_, ob_, oc_ = token_major(o1, 0, 0), token_major(o2, 1, 2), token_major(o3, 2, 3)
    mx = jnp.maximum(jnp.maximum(la, lb), lc)
    ea, eb, ec = jnp.exp(la - mx), jnp.exp(lb - mx), jnp.exp(lc - mx)
    oa = (ea * oa_ + eb * ob_ + ec * oc_) / (ea + eb + ec)
    pa = jnp.dot(oa.astype(BF16), wpa_ref[...], preferred_element_type=F32)
    pb = jnp.dot(ob_ref[...], wpb_ref[...], preferred_element_type=F32)
    ga = jax.nn.sigmoid(gate_ref[:, 0:D_MODEL].astype(F32))
    gb = jax.nn.sigmoid(gate_ref[:, D_MODEL:2 * D_MODEL].astype(F32))
    merged = ga * pa + gb * pb
    y = jnp.dot(merged.astype(BF16), wo_ref[...], preferred_element_type=F32)
    x1 = x_ref[...] + g1_ref[0] * y
    x1_ref[...] = x1
    h2 = _modulated_norm(x1, n2_ref[...], sc2_ref[0], sh2_ref[0]).astype(BF16)
    h2_ref[...] = h2
    q = jnp.dot(h2, wq_ref[...], preferred_element_type=F32).astype(BF16)
    for hp in range(2 * PEER_HEADS):
        sct_ref[hp] = lax.dot_general(sk_ref[hp % 2], q[:, hp * PEER_DK:(hp + 1) * PEER_DK], NT_DIMS,
                                      preferred_element_type=F32)


def _merge_call(os_, ls_, ob, gates, x2, mod_l, n2, wpa, wpb, wo, wq, sk, seq, tm):
    t, d = x2.shape
    nper = seq // tm
    mod_spec = lambda k: pl.BlockSpec((1, 1, d), lambda i: (i // nper, 0, k))
    row = lambda w: pl.BlockSpec((tm, w), lambda i: (i, 0))
    full = lambda a: _resident(a.shape, lambda i: (0,) * a.ndim)
    dil_spec = lambda g: pl.BlockSpec((1, tm // DIL_GROUPS[g][1], DIL_GROUPS[g][1] * GROUP_W),
                                      lambda i: (i // nper, i % nper, 0))
    return pl.pallas_call(
        _merge_kernel,
        grid=(t // tm,),
        in_specs=[dil_spec(0), dil_spec(1), dil_spec(2)] * 2
        + [row(DIFF_W), row(2 * D_MODEL), row(d), mod_spec(2), mod_spec(3), mod_spec(4),
                                       pl.BlockSpec((1, d), lambda i: (0, 0)),
                                       full(wpa), full(wpb), full(wo), full(wq), full(sk)],
        out_specs=[row(d), row(d), pl.BlockSpec((2 * PEER_HEADS, N_KEYS, tm), lambda i: (0, 0, i))],
        out_shape=[jax.ShapeDtypeStruct((t, d), F32), jax.ShapeDtypeStruct((t, d), BF16),
                   jax.ShapeDtypeStruct((2 * PEER_HEADS, N_KEYS, t), F32)],
        scratch_shapes=[pltpu.VMEM((4, GROUP_W // LANES, tm, LANES), F32)],
        compiler_params=_params(("arbitrary",)),
        name="merge_proj",
    )(*os_, *ls_, ob, gates, x2, mod_l, mod_l, mod_l, n2, wpa, wpb, wo, wq, sk)


def _colmax(a):
    return jnp.max(a, axis=0, keepdims=True)


def _bf16_pair_words(x):
    hi = pltpu.bitcast(x.astype(BF16).astype(F32), jnp.uint32)
    return hi | (hi >> 16)


def _bf16_half_words(x):
    n = x.shape[0] // 2
    bits = pltpu.bitcast(x.astype(BF16).astype(F32), jnp.uint32)
    return (bits[0:n] >> 16) | (bits[n:2 * n] & jnp.uint32(0xFFFF0000))


def _second_key_order(table):
    e, d = table.shape
    half = N_KEYS // 2
    return table.reshape(e // N_KEYS, 2, half, d).transpose(0, 2, 1, 3).reshape(e, d)


def _packed_rows(word_row, rows):
    v = pltpu.bitcast(jnp.broadcast_to(word_row, (8, word_row.shape[-1])), BF16)
    return jnp.tile(v, (rows // v.shape[0], 1))


def _topk_kernel(sct_ref, n_ref, c_ref, r_ref, w_ref):
    tl = sct_ref.shape[-1]
    k = PEER_TOPK
    row16 = lax.broadcasted_iota(jnp.int32, (k, tl), 0)
    for h in range(PEER_HEADS):
        s0 = sct_ref[2 * h]
        s1 = sct_ref[2 * h + 1]

        def top_sorted(s, want_rank):
            work = s
            vals = jnp.full((k, tl), NEG_INF, F32)
            rank = jnp.full(s.shape, float(k), F32)
            for rr in range(k):
                v = _colmax(work)
                eq = work == v
                if want_rank:
                    rank = jnp.where(eq, float(rr), rank)
                work = jnp.where(eq, NEG_INF, work)
                vals = jnp.where(row16 == rr, v, vals)
            return vals, rank

        a, _ = top_sorted(s0, False)
        b, r1 = top_sorted(s1, True)
        cands = []
        for x in range(k):
            ny = k // (x + 1)
            cands.append(jnp.where(row16 < ny, a[x:x + 1] + b, NEG_INF))
        work = list(cands)
        c16 = None
        for rr in range(k):
            v = work[0]
            for w_ in work[1:]:
                v = jnp.maximum(v, w_)
            v = _colmax(v)
            work = [jnp.where(w_ == v, NEG_INF, w_) for w_ in work]
            c16 = v
        smax = a[0:1] + b[0:1]
        z = jnp.zeros((1, tl), F32)
        for cd in cands:
            z = z + jnp.sum(jnp.where(cd >= c16, jnp.exp(cd - smax), 0.0), axis=0, keepdims=True)
        n = jnp.zeros(s0.shape, F32)
        for y in range(k):
            n = n + jnp.where(s0 + b[y:y + 1] >= c16, 1.0, 0.0)
        n_ref[h] = _bf16_pair_words(n)
        c_ref[h] = _bf16_pair_words(jnp.exp(s0 - a[0:1]) / z)
        r_ref[h] = _bf16_half_words(r1)
        w_ref[h] = _bf16_half_words(jnp.exp(s1 - b[0:1]))


def _topk_call(sct, tl):
    hp, nk, t = sct.shape
    spec = lambda rows: pl.BlockSpec((PEER_HEADS, rows, tl), lambda i: (0, 0, i))
    return pl.pallas_call(
        _topk_kernel,
        grid=(t // tl,),
        in_specs=[pl.BlockSpec((hp, nk, tl), lambda i: (0, 0, i))],
        out_specs=[spec(nk), spec(nk), spec(nk // 2), spec(nk // 2)],
        out_shape=[jax.ShapeDtypeStruct((PEER_HEADS, nk, t), jnp.uint32)] * 2
        + [jax.ShapeDtypeStruct((PEER_HEADS, nk // 2, t), jnp.uint32)] * 2,
        compiler_params=_params(("arbitrary",)),
        name="peer_topk",
    )(sct)


def _gelu_tanh(x):
    return 0.5 * x * (1.0 + jnp.tanh(math.sqrt(2.0 / math.pi) * (x + 0.044715 * (x * x * x))))


PEER_COLS = 512


def _peer_kernel(h2_ref, u_ref, vt_ref, n_ref, c_ref, r_ref, w_ref, x1_ref, g2_ref, o_ref, acc_s, ga_s, *, n_first):
    e = pl.program_id(1)
    tm = h2_ref.shape[0]

    @pl.when(e == 0)
    def _():
        acc_s[...] = jnp.zeros(acc_s.shape, F32)

    for cb in range(tm // PEER_COLS):
        blk = slice(cb * PEER_COLS, (cb + 1) * PEER_COLS)
        pre = lax.dot_general(u_ref[...], h2_ref[blk, :], NT_DIMS, preferred_element_type=F32)
        act = _gelu_tanh(pre.astype(BF16))
        for ch in range(PEER_COLS // LANES):
            cols = slice(cb * PEER_COLS + ch * LANES, cb * PEER_COLS + (ch + 1) * LANES)
            for ip in range(n_first // 2):
                gts = [jnp.zeros((N_KEYS, LANES), BF16), jnp.zeros((N_KEYS, LANES), BF16)]
                for h in range(PEER_HEADS):
                    r_t = pltpu.bitcast(r_ref[h, :, cols], BF16)
                    w_t = pltpu.bitcast(w_ref[h, :, cols], BF16)
                    for k in range(2):
                        ii = 2 * ip + k
                        n_row = _packed_rows(n_ref[h, ii:ii + 1, cols], N_KEYS)
                        c_row = _packed_rows(c_ref[h, ii:ii + 1, cols], N_KEYS)
                        gts[k] = gts[k] + jnp.where(r_t < n_row, w_t * c_row, jnp.zeros_like(w_t))
                for k in range(2):
                    rows = slice((2 * ip + k) * N_KEYS, (2 * ip + k + 1) * N_KEYS)
                    ga_s[rows, cols] = gts[k] * act[rows, ch * LANES:(ch + 1) * LANES]
        acc_s[:, blk] += jnp.dot(vt_ref[...], ga_s[:, blk], preferred_element_type=F32)

    @pl.when(e == pl.num_programs(1) - 1)
    def _():
        o_ref[...] = x1_ref[...] + g2_ref[0] * acc_s[...].T


def _peer_call(h2, u_bf, vt_bf, n, c, r, w, x1, mod_l, seq, tm, te):
    t, d = x1.shape
    nper = seq // tm
    n_first = te // N_KEYS
    sel = pl.BlockSpec((PEER_HEADS, N_KEYS // 2, tm), lambda i, e: (0, 0, i))
    first = pl.BlockSpec((PEER_HEADS, n_first, tm), lambda i, e: (0, e, i))
    return pl.pallas_call(
        functools.partial(_peer_kernel, n_first=n_first),
        grid=(t // tm, u_bf.shape[0] // te),
        in_specs=[pl.BlockSpec((tm, d), lambda i, e: (i, 0)),
                  pl.BlockSpec((te, d), lambda i, e: (e, 0)),
                  pl.BlockSpec((d, te), lambda i, e: (0, e)),
                  first, first, sel, sel,
                  pl.BlockSpec((tm, d), lambda i, e: (i, 0)),
                  pl.BlockSpec((1, 1, d), lambda i, e: (i // nper, 0, 5))],
        out_specs=pl.BlockSpec((tm, d), lambda i, e: (i, 0)),
        out_shape=jax.ShapeDtypeStruct((t, d), F32),
        scratch_shapes=[pltpu.VMEM((d, tm), F32), pltpu.VMEM((te, tm), BF16)],
        compiler_params=_params(("arbitrary", "arbitrary")),
        name="peer_dense",
    )(h2, u_bf, vt_bf, n, c, r, w, x1, mod_l)


def _final_kernel(x_ref, g_ref, o_ref):
    x = x_ref[...]
    ms = jnp.mean(x * x, axis=-1, keepdims=True)
    o_ref[...] = x * lax.rsqrt(ms + NORM_EPS) * g_ref[...]


def _final_call(x2, g, tm):
    t, d = x2.shape
    return pl.pallas_call(
        _final_kernel,
        grid=(t // tm,),
        in_specs=[pl.BlockSpec((tm, d), lambda i: (i, 0)), pl.BlockSpec((1, d), lambda i: (0, 0))],
        out_specs=pl.BlockSpec((tm, d), lambda i: (i, 0)),
        out_shape=jax.ShapeDtypeStruct((t, d), F32),
        compiler_params=_params(("arbitrary",)),
        name="final_norm",
    )(x2, g)


TOKEN_TILE = 512
DIFF_TILE = 512
TOPK_LANES = 128
EXPERT_TILE = 1024


def _mixer_layer(x2, mod_l, layer, lam_init, batch, seq, w_in_bf, w_vt_bf, norm1_g, dil_bias, diff_bias, lam_vecs,
                 subln_g, norm2_g, wpa, wpb, wo, wq, sk):
    p1, p2, p3, qk, vt, gates = _in_call(x2, mod_l, norm1_g, w_in_bf, w_vt_bf, seq, TOKEN_TILE)
    os_, ls_ = [], []
    for g, p in enumerate((p1, p2, p3)):
        o, lse = _dil_call(p, dil_bias[g], batch, seq, DIL_GROUPS[g][1])
        os_.append(o)
        ls_.append(lse)
    ob = _diff_call(qk, vt, diff_bias, lam_vecs, subln_g, batch, seq, DIFF_TILE, lam_init)
    return _merge_call(os_, ls_, ob, gates, x2, mod_l, norm2_g, wpa, wpb, wo, wq, sk, seq, TOKEN_TILE)


def _peer_layer(x1, h2, sct, mod_l, u_bf, vt_bf, seq):
    n, c, r, w = _topk_call(sct, TOPK_LANES)
    return _peer_call(h2, u_bf, vt_bf, n, c, r, w, x1, mod_l, seq, TOKEN_TILE, EXPERT_TILE)


def kernel(x, c, w_ada, b_ada, norm1_g, norm2_g, w_in, w_proj_a, w_proj_b, w_out, lam_q1, lam_k1, lam_q2, lam_k2,
           subln_g, rel_bias, peer_wq, peer_subkeys, peer_u, peer_v, final_g):
    batch, seq, d = x.shape
    depth = w_ada.shape[0]
    x2 = x.reshape(batch * seq, d)
    mods = _ada_call(c, w_ada, b_ada)
    dil_bias = [_dil_bias_table(rel_bias, g, DIL_GROUPS[g][1]) for g in range(N_DIL_GROUPS)]
    diff_bias = _diff_bias_table(rel_bias, DIFF_TILE, seq // DIFF_TILE)
    for l in range(depth):
        lam_init = 0.8 - 0.6 * math.exp(-0.3 * l)
        mod_l = mods[l].reshape(batch, 1, 6 * d)
        lam_vecs = jnp.stack([lam_q1[l], lam_k1[l], lam_q2[l], lam_k2[l]])
        v_cols = slice(3 * DIL_WIDTH + 2 * DIFF_W, 3 * DIL_WIDTH + 3 * DIFF_W)
        x1, h2, sct = _mixer_layer(
            x2, mod_l, l, lam_init, batch, seq, w_in[l].astype(BF16), w_in[l][:, v_cols].T.astype(BF16),
            norm1_g[l].reshape(1, d), dil_bias, diff_bias,
            lam_vecs, subln_g[l].reshape(2 * HEAD_DIM, 1), norm2_g[l].reshape(1, d), w_proj_a[l].astype(BF16),
            w_proj_b[l].astype(BF16), w_out[l].astype(BF16), peer_wq[l].astype(BF16), peer_subkeys[l].astype(BF16))
        x2 = _peer_layer(x1, h2, sct, mod_l, _second_key_order(peer_u[l]).astype(BF16),
                         _second_key_order(peer_v[l]).T.astype(BF16), seq)
    return _final_call(x2, final_g.reshape(1, d), TOKEN_TILE).reshape(batch, seq, d)
```
